```python
import math
import jax, jax.numpy as jnp
from jax import lax
import numpy as np


D_MODEL = 4096
BATCH = 4
SEQ = 4096
DEPTH = 4

POOL_WIDTH = D_MODEL // 2
POOL_WINDOWS = (2, 4, 8, 16)
POOL_GROUP = POOL_WIDTH // len(POOL_WINDOWS)
SCONV_WIDTH = D_MODEL // 2
SCONV_K = 3
EVEN_IN = POOL_WIDTH + 3 * SCONV_WIDTH
EVEN_OUT = POOL_WIDTH + SCONV_WIDTH
HEAD_DIM = 64
N_Q_HEADS = D_MODEL // 2 // HEAD_DIM
N_KV_HEADS = N_Q_HEADS // 4
Q_PER_KV = N_Q_HEADS // N_KV_HEADS
WINDOW = 128
BLOCK = 128
Q_WIDTH = N_Q_HEADS * HEAD_DIM
KV_WIDTH = N_KV_HEADS * HEAD_DIM
SSM_WIDTH = D_MODEL // 4
SSM_GROUP_CH = 16
SSM_GROUPS = SSM_WIDTH // SSM_GROUP_CH
SSM_STATE = 64
ODD_IN = Q_WIDTH + 2 * KV_WIDTH + SSM_WIDTH
ODD_OUT = Q_WIDTH + SSM_WIDTH
REL_BUCKETS = 32
REL_MAX_DIST = 128
D_FF = ((8 * D_MODEL // 3 + 255) // 256) * 256
FFN_K = 3
EPS = 1e-6
N_EVEN = (DEPTH + 1) // 2
N_ODD = DEPTH // 2

kernel_name = 'hybrid_pool_conv_swa_s5_trunk'


def rms_norm(x, g):
    xf = x.astype(jnp.float32)
    y = xf * lax.rsqrt(jnp.mean(xf * xf, axis=-1, keepdims=True) + EPS)
    return (y * g.astype(jnp.float32)).astype(x.dtype)


def causal_dwconv(z, w):
    k, c = w.shape
    return lax.conv_general_dilated(
        z, w[:, None, :].astype(z.dtype), window_strides=(1,), padding=[(k - 1, 0)],
        dimension_numbers=('NWC', 'WIO', 'NWC'), feature_group_count=c)


def pool_mixer(a, w_grp, scale):
    L = a.shape[1]
    af = a.astype(jnp.float32)
    t = jnp.arange(1, L + 1, dtype=jnp.float32)[None, :, None]
    outs = []
    for gi, w in enumerate(POOL_WINDOWS):
        ag = af[..., gi * POOL_GROUP:(gi + 1) * POOL_GROUP]
        cs = jnp.cumsum(ag, axis=1)
        shifted = jnp.pad(cs, ((0, 0), (w, 0), (0, 0)))[:, :L]
        outs.append((cs - shifted) / jnp.minimum(t, float(w)) - ag)
    p = jnp.stack(outs, axis=2).astype(a.dtype)
    y = jnp.einsum('blgc,gcd->blgd', p, w_grp)
    return y.reshape(a.shape) * scale


def t5_causal_bucket(n):
    n = jnp.maximum(n, 0)
    max_exact = REL_BUCKETS // 2
    nf = jnp.maximum(n, 1).astype(jnp.float32)
    large = max_exact + (jnp.log(nf / max_exact) / math.log(REL_MAX_DIST / max_exact)
                         * (REL_BUCKETS - max_exact)).astype(jnp.int32)
    large = jnp.minimum(large, REL_BUCKETS - 1)
    return jnp.where(n < max_exact, n, large)


def swa_sink_attention(q, k, v, sinks, rel_bias):
    B, L = q.shape[:2]
    nb = L // BLOCK
    qb = q.reshape(B, nb, BLOCK, N_KV_HEADS, Q_PER_KV, HEAD_DIM)

    def band(t):
        tb = t.reshape(B, nb, BLOCK, N_KV_HEADS, HEAD_DIM)
        prev = jnp.pad(tb, ((0, 0), (1, 0), (0, 0), (0, 0), (0, 0)))[:, :nb]
        return jnp.concatenate([prev, tb], axis=2)

    kb, vb = band(k), band(v)
    s = jnp.einsum('bnqhgd,bnkhd->bnhgqk', qb, kb,
                   preferred_element_type=jnp.float32) * (HEAD_DIM ** -0.5)
    qi = jnp.arange(BLOCK)[:, None]
    kj = jnp.arange(2 * BLOCK)[None, :]
    dist = qi + BLOCK - kj
    bias = rel_bias[t5_causal_bucket(dist)].astype(jnp.float32)
    bias = bias.transpose(2, 0, 1).reshape(N_KV_HEADS, Q_PER_KV, BLOCK, 2 * BLOCK)
    blk = jnp.arange(nb)[:, None, None]
    valid = (dist >= 0) & (dist < WINDOW) & (blk * BLOCK + kj - BLOCK >= 0)
    s = jnp.where(valid[None, :, None, None], s + bias, -jnp.inf)
    sink = sinks.astype(jnp.float32).reshape(N_KV_HEADS, Q_PER_KV)[None, None, :, :, None, None]
    m = jnp.maximum(jnp.max(s, axis=-1, keepdims=True), sink)
    p = jnp.exp(s - m)
    p = p / (jnp.sum(p, axis=-1, keepdims=True) + jnp.exp(sink - m))
    o = jnp.einsum('bnhgqk,bnkhd->bnqhgd', p.astype(v.dtype), vb)
    return o.reshape(B, L, Q_WIDTH)


def s5_ssm_glu(u, a_re, a_im, log_step, b_re, b_im, c_re, c_im, d_skip, w_glu, b_glu):
    Bsz, L = u.shape[:2]
    f32 = jnp.float32
    uf = u.astype(f32).reshape(Bsz, L, SSM_GROUPS, SSM_GROUP_CH)
    a_re, a_im = a_re.astype(f32), a_im.astype(f32)
    b_re, b_im = b_re.astype(f32), b_im.astype(f32)
    c_re, c_im = c_re.astype(f32), c_im.astype(f32)
    dt = jnp.exp(log_step.astype(f32))[:, None]
    mag = jnp.exp(a_re * dt)
    lb_re, lb_im = mag * jnp.cos(a_im * dt), mag * jnp.sin(a_im * dt)
    den = a_re * a_re + a_im * a_im
    n_re, n_im = lb_re - 1.0, lb_im
    f_re = ((n_re * a_re + n_im * a_im) / den)[..., None]
    f_im = ((n_im * a_re - n_re * a_im) / den)[..., None]
    bb_re = f_re * b_re - f_im * b_im
    bb_im = f_re * b_im + f_im * b_re
    bu_re = jnp.einsum('blgh,gph->blgp', uf, bb_re)
    bu_im = jnp.einsum('blgh,gph->blgp', uf, bb_im)
    lam_re = jnp.broadcast_to(lb_re, bu_re.shape)
    lam_im = jnp.broadcast_to(lb_im, bu_re.shape)

    def combine(e1, e2):
        a1r, a1i, b1r, b1i = e1
        a2r, a2i, b2r, b2i = e2
        return (a2r * a1r - a2i * a1i, a2r * a1i + a2i * a1r,
                a2r * b1r - a2i * b1i + b2r, a2r * b1i + a2i * b1r + b2i)

    _, _, x_re, x_im = lax.associative_scan(combine, (lam_re, lam_im, bu_re, bu_im), axis=1)
    y = jnp.einsum('blgp,ghp->blgh', x_re, c_re) - jnp.einsum('blgp,ghp->blgh', x_im, c_im)
    y = (y + d_skip.astype(f32).reshape(SSM_GROUPS, SSM_GROUP_CH) * uf).reshape(Bsz, L, SSM_WIDTH)
    g = jax.nn.gelu(y)
    out = g * jax.nn.sigmoid(g @ w_glu.astype(f32) + b_glu.astype(f32))
    return out.astype(u.dtype)


def even_mixer(h, w_in, pool_w, pool_scale, conv_w, w_out):
    z = h @ w_in
    a = z[..., :POOL_WIDTH]
    hb = z[..., POOL_WIDTH:POOL_WIDTH + SCONV_WIDTH]
    gb = z[..., POOL_WIDTH + SCONV_WIDTH:POOL_WIDTH + 2 * SCONV_WIDTH]
    gc = z[..., POOL_WIDTH + 2 * SCONV_WIDTH:]
    ya = pool_mixer(a, pool_w, pool_scale)
    yb = gb * causal_dwconv(gc * hb, conv_w)
    return jnp.concatenate([ya, yb], axis=-1) @ w_out


def odd_mixer(h, w_in, sinks, rel_bias, a_re, a_im, log_step, b_re, b_im, c_re, c_im,
              d_skip, w_glu, b_glu, w_out):
    B, L, _ = h.shape
    z = h @ w_in
    q = z[..., :Q_WIDTH].reshape(B, L, N_Q_HEADS, HEAD_DIM)
    k = z[..., Q_WIDTH:Q_WIDTH + KV_WIDTH].reshape(B, L, N_KV_HEADS, HEAD_DIM)
    v = z[..., Q_WIDTH + KV_WIDTH:Q_WIDTH + 2 * KV_WIDTH].reshape(B, L, N_KV_HEADS, HEAD_DIM)
    u = z[..., Q_WIDTH + 2 * KV_WIDTH:]
    ya = swa_sink_attention(q, k, v, sinks, rel_bias)
    ys = s5_ssm_glu(u, a_re, a_im, log_step, b_re, b_im, c_re, c_im, d_skip, w_glu, b_glu)
    return jnp.concatenate([ya, ys], axis=-1) @ w_out


def conv_glu_ffn(h, w_up, conv_w, w_down):
    z = causal_dwconv(h @ w_up, conv_w)
    gate, up = z[..., :D_FF], z[..., D_FF:]
    return (jax.nn.silu(gate) * up) @ w_down


def setup_inputs(seed: int = 0) -> dict:
    key = jax.random.key(seed)
    ks = jax.random.split(key, 32)
    f32 = jnp.float32

    def nrm(k, shape, scale):
        return jax.random.normal(k, shape, f32) * scale

    def gain(k, shape):
        return 1.0 + 0.02 * jax.random.normal(k, shape, f32)

    a_im_base = math.pi * jnp.arange(SSM_STATE, dtype=f32)
    return {
        'x': nrm(ks[0], (BATCH, SEQ, D_MODEL), 1.0),
        'rel_bias': nrm(ks[1], (REL_BUCKETS, N_Q_HEADS), 0.5),
        'norm_mix_pre': gain(ks[2], (DEPTH, D_MODEL)),
        'norm_mix_post': gain(ks[3], (DEPTH, D_MODEL)),
        'norm_ffn_pre': gain(ks[4], (DEPTH, D_MODEL)),
        'norm_ffn_post': gain(ks[5], (DEPTH, D_MODEL)),
        'e_w_in': nrm(ks[6], (N_EVEN, D_MODEL, EVEN_IN), D_MODEL ** -0.5),
        'e_pool_w': nrm(ks[7], (N_EVEN, len(POOL_WINDOWS), POOL_GROUP, POOL_GROUP), POOL_GROUP ** -0.5),
        'e_pool_scale': gain(ks[8], (N_EVEN, POOL_WIDTH)),
        'e_conv_w': nrm(ks[9], (N_EVEN, SCONV_K, SCONV_WIDTH), SCONV_K ** -0.5),
        'e_w_out': nrm(ks[10], (N_EVEN, EVEN_OUT, D_MODEL), EVEN_OUT ** -0.5),
        'o_w_in': nrm(ks[11], (N_ODD, D_MODEL, ODD_IN), D_MODEL ** -0.5),
        'o_sinks': nrm(ks[12], (N_ODD, N_Q_HEADS), 1.0),
        'o_a_re': -0.5 + nrm(ks[13], (N_ODD, SSM_GROUPS, SSM_STATE), 0.01),
        'o_a_im': a_im_base + nrm(ks[14], (N_ODD, SSM_GROUPS, SSM_STATE), 0.01),
        'o_log_step': jax.random.uniform(ks[15], (N_ODD, SSM_GROUPS), f32,
                                         minval=math.log(1e-3), maxval=math.log(1e-1)),
        'o_b_re': nrm(ks[16], (N_ODD, SSM_GROUPS, SSM_STATE, SSM_GROUP_CH), (2 * SSM_GROUP_CH) ** -0.5),
        'o_b_im': nrm(ks[17], (N_ODD, SSM_GROUPS, SSM_STATE, SSM_GROUP_CH), (2 * SSM_GROUP_CH) ** -0.5),
        'o_c_re': nrm(ks[18], (N_ODD, SSM_GROUPS, SSM_GROUP_CH, SSM_STATE), (2 * SSM_STATE) ** -0.5),
        'o_c_im': nrm(ks[19], (N_ODD, SSM_GROUPS, SSM_GROUP_CH, SSM_STATE), (2 * SSM_STATE) ** -0.5),
        'o_d': nrm(ks[20], (N_ODD, SSM_WIDTH), 1.0),
        'o_glu_w': nrm(ks[21], (N_ODD, SSM_WIDTH, SSM_WIDTH), SSM_WIDTH ** -0.5),
        'o_glu_b': nrm(ks[22], (N_ODD, SSM_WIDTH), 0.02),
        'o_w_out': nrm(ks[23], (N_ODD, ODD_OUT, D_MODEL), ODD_OUT ** -0.5),
        'f_w_up': nrm(ks[24], (DEPTH, D_MODEL, 2 * D_FF), D_MODEL ** -0.5),
        'f_conv_w': nrm(ks[25], (DEPTH, FFN_K, 2 * D_FF), FFN_K ** -0.5),
        'f_w_down': nrm(ks[26], (DEPTH, D_FF, D_MODEL), D_FF ** -0.5),
    }


def reference(x, rel_bias, norm_mix_pre, norm_mix_post, norm_ffn_pre, norm_ffn_post,
              e_w_in, e_pool_w, e_pool_scale, e_conv_w, e_w_out,
              o_w_in, o_sinks, o_a_re, o_a_im, o_log_step, o_b_re, o_b_im, o_c_re, o_c_im,
              o_d, o_glu_w, o_glu_b, o_w_out,
              f_w_up, f_conv_w, f_w_down):
    for i in range(DEPTH):
        j = i // 2
        h = rms_norm(x, norm_mix_pre[i])
        if i % 2 == 0:
            m = even_mixer(h, e_w_in[j], e_pool_w[j], e_pool_scale[j], e_conv_w[j], e_w_out[j])
        else:
            m = odd_mixer(h, o_w_in[j], o_sinks[j], rel_bias, o_a_re[j], o_a_im[j], o_log_step[j],
                          o_b_re[j], o_b_im[j], o_c_re[j], o_c_im[j], o_d[j], o_glu_w[j],
                          o_glu_b[j], o_w_out[j])
        x = x + rms_norm(m, norm_mix_post[i])
        h = rms_norm(x, norm_ffn_pre[i])
        x = x + rms_norm(conv_glu_ffn(h, f_w_up[i], f_conv_w[i], f_w_down[i]), norm_ffn_post[i])
    return x
```

```python
import functools
import math

import jax
import jax.numpy as jnp
from jax import lax
from jax.experimental import pallas as pl
from jax.experimental.pallas import tpu as pltpu

F32 = jnp.float32
BF16 = jnp.bfloat16

D_MODEL = 4096
BATCH = 4
SEQ = 4096
DEPTH = 4

POOL_WIDTH = D_MODEL // 2
POOL_WINDOWS = (2, 4, 8, 16)
POOL_GROUP = POOL_WIDTH // len(POOL_WINDOWS)
SCONV_WIDTH = D_MODEL // 2
HEAD_DIM = 64
N_Q_HEADS = D_MODEL // 2 // HEAD_DIM
N_KV_HEADS = N_Q_HEADS // 4
Q_PER_KV = N_Q_HEADS // N_KV_HEADS
WINDOW = 128
BLOCK = 128
Q_WIDTH = N_Q_HEADS * HEAD_DIM
KV_WIDTH = N_KV_HEADS * HEAD_DIM
SSM_WIDTH = D_MODEL // 4
SSM_GROUP_CH = 16
SSM_GROUPS = SSM_WIDTH // SSM_GROUP_CH
SSM_STATE = 64
ODD_OUT = Q_WIDTH + SSM_WIDTH
REL_BUCKETS = 32
REL_MAX_DIST = 128
D_FF = ((8 * D_MODEL // 3 + 255) // 256) * 256
EPS = 1e-6

CHUNK = 16
CHUNK_W = CHUNK * SSM_GROUP_CH

HALO = 16
CONV_HALO = 8
NEG = -1e30

TM_NORM = 256
TM_MIX = 512
TM_OUT = 256
TM_FFN = 1024
TM_MM = 1024
TM_GLU = 512
TN_FFN = 256
SSM_GROUP_BLOCK = 4

VMEM_LIMIT = 56 * 1024 * 1024


def _cparams(sem):
    return pltpu.CompilerParams(dimension_semantics=sem, vmem_limit_bytes=VMEM_LIMIT)


def _dot(a, b):
    return jnp.dot(a, b, preferred_element_type=F32)


def _load_history(ext_view, carry_view, is_first, rows):
    @pl.when(is_first)
    def _():
        ext_view[0:rows, :] = jnp.zeros((rows, ext_view.shape[1]), F32)

    @pl.when(jnp.logical_not(is_first))
    def _():
        ext_view[0:rows, :] = carry_view[...]


def _rms_kernel(x_ref, g_ref, o_ref):
    x = x_ref[...]
    r = lax.rsqrt(jnp.mean(x * x, axis=-1, keepdims=True) + EPS)
    o_ref[...] = (x * r * g_ref[...]).astype(o_ref.dtype)


def rms_norm_bf16(x, g):
    m, d = x.shape
    tm = min(TM_NORM, m)
    return pl.pallas_call(
        _rms_kernel,
        grid=(m // tm,),
        in_specs=[pl.BlockSpec((tm, d), lambda i: (i, 0)),
                  pl.BlockSpec((1, d), lambda i: (0, 0))],
        out_specs=pl.BlockSpec((tm, d), lambda i: (i, 0)),
        out_shape=jax.ShapeDtypeStruct((m, d), BF16),
        compiler_params=_cparams(("arbitrary",)),
        name="rms_norm",
    )(x, g.reshape(1, d))


def _pool_kernel(h_ref, w_ref, pw_ref, sc_ref, o_ref, carry_ref, ext_ref, *, tm, tiles_per_seq):
    i = pl.program_id(0)
    j = pl.program_id(1)
    seq_tile = i % tiles_per_seq
    _load_history(ext_ref, carry_ref.at[j], seq_tile == 0, HALO)
    a = _dot(h_ref[...], w_ref[...])
    ext_ref[HALO:, :] = a
    carry_ref[j] = a[tm - HALO:, :]
    t = seq_tile * tm + lax.broadcasted_iota(jnp.int32, (tm, 1), 0)
    for gi, w in enumerate(POOL_WINDOWS):
        @pl.when(j == gi)
        def _():
            cur = ext_ref[HALO:, :]
            s = cur
            for d in range(1, w):
                s = s + ext_ref[HALO - d:HALO - d + tm, :]
            den = jnp.minimum(t + 1, w).astype(F32)
            p = s / den - cur
            y = _dot(p.astype(BF16), pw_ref[...]) * sc_ref[...]
            o_ref[...] = y.astype(o_ref.dtype)


def pool_mixer(h, w_in, pool_w, pool_scale):
    m, d = h.shape
    tn = POOL_GROUP
    tm = min(TM_MIX, SEQ)
    ng = len(POOL_WINDOWS)
    return pl.pallas_call(
        functools.partial(_pool_kernel, tm=tm, tiles_per_seq=SEQ // tm),
        grid=(m // tm, ng),
        in_specs=[pl.BlockSpec((tm, d), lambda i, j: (i, 0)),
                  pl.BlockSpec((d, tn), lambda i, j: (0, j)),
                  pl.BlockSpec((None, tn, tn), lambda i, j: (j, 0, 0)),
                  pl.BlockSpec((1, tn), lambda i, j: (0, j))],
        out_specs=pl.BlockSpec((tm, tn), lambda i, j: (i, j)),
        out_shape=jax.ShapeDtypeStruct((m, POOL_WIDTH + SCONV_WIDTH), BF16),
        scratch_shapes=[pltpu.VMEM((ng, HALO, tn), F32),
                        pltpu.VMEM((tm + HALO, tn), F32)],
        compiler_params=_cparams(("arbitrary", "arbitrary")),
        name="pool_mixer",
    )(h, w_in, pool_w, pool_scale.reshape(1, POOL_WIDTH))


def _sconv_kernel(h_ref, whb_ref, wgb_ref, wgc_ref, cw_ref, y_in_ref, o_ref, carry_ref, ext_ref,
                  *, tm, tiles_per_seq):
    del y_in_ref
    i = pl.program_id(0)
    j = pl.program_id(1)
    _load_history(ext_ref, carry_ref.at[j], i % tiles_per_seq == 0, CONV_HALO)
    h = h_ref[...]
    u = _dot(h, wgc_ref[...]) * _dot(h, whb_ref[...])
    ext_ref[CONV_HALO:, :] = u
    carry_ref[j] = u[tm - CONV_HALO:, :]
    cw = cw_ref[...]
    conv = (cw[2:3, :] * ext_ref[CONV_HALO:, :]
            + cw[1:2, :] * ext_ref[CONV_HALO - 1:CONV_HALO - 1 + tm, :]
            + cw[0:1, :] * ext_ref[CONV_HALO - 2:CONV_HALO - 2 + tm, :])
    o_ref[...] = (_dot(h, wgb_ref[...]) * conv).astype(o_ref.dtype)


def sconv_mixer(h, w_in, conv_w, y_buf):
    m, d = h.shape
    tn = min(512, SCONV_WIDTH)
    tm = min(TM_MIX, SEQ)
    nb = SCONV_WIDTH // tn
    off = POOL_WIDTH // tn
    return pl.pallas_call(
        functools.partial(_sconv_kernel, tm=tm, tiles_per_seq=SEQ // tm),
        grid=(m // tm, nb),
        in_specs=[pl.BlockSpec((tm, d), lambda i, j: (i, 0)),
                  pl.BlockSpec((d, tn), lambda i, j: (0, off + j)),
                  pl.BlockSpec((d, tn), lambda i, j: (0, off + nb + j)),
                  pl.BlockSpec((d, tn), lambda i, j: (0, off + 2 * nb + j)),
                  pl.BlockSpec((3, tn), lambda i, j: (0, j)),
                  pl.BlockSpec(memory_space=pl.ANY)],
        out_specs=pl.BlockSpec((tm, tn), lambda i, j: (i, off + j)),
        out_shape=jax.ShapeDtypeStruct(y_buf.shape, y_buf.dtype),
        input_output_aliases={5: 0},
        scratch_shapes=[pltpu.VMEM((nb, CONV_HALO, tn), F32),
                        pltpu.VMEM((tm + CONV_HALO, tn), F32)],
        compiler_params=_cparams(("arbitrary", "arbitrary")),
        name="sconv_mixer",
    )(h, w_in, w_in, w_in, conv_w, y_buf)


def _outproj_kernel(y_ref, w_ref, x_ref, gpost_ref, gpre_ref, xo_ref, *rest, nk, emit_h):
    if emit_h:
        ho_ref, acc_ref = rest
    else:
        (acc_ref,) = rest
    k = pl.program_id(1)

    @pl.when(k == 0)
    def _():
        acc_ref[...] = jnp.zeros_like(acc_ref)

    acc_ref[...] += _dot(y_ref[...], w_ref[...])

    @pl.when(k == nk - 1)
    def _():
        mix = acc_ref[...]
        r = lax.rsqrt(jnp.mean(mix * mix, axis=-1, keepdims=True) + EPS)
        xn = x_ref[...] + mix * r * gpost_ref[...]
        xo_ref[...] = xn
        if emit_h:
            r2 = lax.rsqrt(jnp.mean(xn * xn, axis=-1, keepdims=True) + EPS)
            ho_ref[...] = (xn * r2 * gpre_ref[...]).astype(ho_ref.dtype)


def outproj_residual_norm(y, w, x, g_post, g_pre_next):
    m, kdim = y.shape
    d = w.shape[1]
    tm = min(TM_OUT, m)
    tk = 512 if kdim % 512 == 0 else 256
    nk = kdim // tk
    emit_h = g_pre_next is not None
    if not emit_h:
        g_pre_next = g_post
    out_shape = [jax.ShapeDtypeStruct((m, d), F32)]
    out_specs = [pl.BlockSpec((tm, d), lambda i, k: (i, 0))]
    if emit_h:
        out_shape.append(jax.ShapeDtypeStruct((m, d), BF16))
        out_specs.append(pl.BlockSpec((tm, d), lambda i, k: (i, 0)))
    res = pl.pallas_call(
        functools.partial(_outproj_kernel, nk=nk, emit_h=emit_h),
        grid=(m // tm, nk),
        in_specs=[pl.BlockSpec((tm, tk), lambda i, k: (i, k)),
                  pl.BlockSpec((tk, d), lambda i, k: (k, 0)),
                  pl.BlockSpec((tm, d), lambda i, k: (i, 0)),
                  pl.BlockSpec((1, d), lambda i, k: (0, 0)),
                  pl.BlockSpec((1, d), lambda i, k: (0, 0))],
        out_specs=out_specs,
        out_shape=out_shape,
        scratch_shapes=[pltpu.VMEM((tm, d), F32)],
        compiler_params=_cparams(("arbitrary", "arbitrary")),
        name="outproj_norm",
    )(y, w, x, g_post.reshape(1, d), g_pre_next.reshape(1, d))
    return (res[0], res[1]) if emit_h else (res[0], None)


def _ffn_up_kernel(h_ref, wg_ref, wu_ref, cg_ref, cu_ref, o_ref, carry_ref, ext_ref,
                   *, tm, tiles_per_seq):
    i = pl.program_id(0)
    j = pl.program_id(1)
    first = i % tiles_per_seq == 0
    h = h_ref[...]

    def conv(w_ref, cw_ref, slot):
        ext = ext_ref.at[slot]
        _load_history(ext, carry_ref.at[j, slot], first, CONV_HALO)
        z = _dot(h, w_ref[...])
        ext[CONV_HALO:, :] = z
        carry_ref[j, slot] = z[tm - CONV_HALO:, :]
        cw = cw_ref[...]
        return (cw[2:3, :] * ext[CONV_HALO:, :]
                + cw[1:2, :] * ext[CONV_HALO - 1:CONV_HALO - 1 + tm, :]
                + cw[0:1, :] * ext[CONV_HALO - 2:CONV_HALO - 2 + tm, :])

    gate = conv(wg_ref, cg_ref, 0)
    up = conv(wu_ref, cu_ref, 1)
    o_ref[...] = (gate * (1.0 / (1.0 + jnp.exp(-gate))) * up).astype(o_ref.dtype)


def ffn_up(h, w_up, conv_w):
    m, d = h.shape
    tn = TN_FFN
    tm = min(TM_FFN, SEQ)
    nj = D_FF // tn
    return pl.pallas_call(
        functools.partial(_ffn_up_kernel, tm=tm, tiles_per_seq=SEQ // tm),
        grid=(m // tm, nj),
        in_specs=[pl.BlockSpec((tm, d), lambda i, j: (i, 0)),
                  pl.BlockSpec((d, tn), lambda i, j: (0, j)),
                  pl.BlockSpec((d, tn), lambda i, j: (0, nj + j)),
                  pl.BlockSpec((3, tn), lambda i, j: (0, j)),
                  pl.BlockSpec((3, tn), lambda i, j: (0, nj + j))],
        out_specs=pl.BlockSpec((tm, tn), lambda i, j: (i, j)),
        out_shape=jax.ShapeDtypeStruct((m, D_FF), BF16),
        scratch_shapes=[pltpu.VMEM((nj, 2, CONV_HALO, tn), F32),
                        pltpu.VMEM((2, tm + CONV_HALO, tn), F32)],
        compiler_params=_cparams(("arbitrary", "arbitrary")),
        name="ffn_up",
    )(h, w_up, w_up, conv_w, conv_w)


def _mm_kernel(a_ref, w_ref, o_ref):
    o_ref[...] = _dot(a_ref[...], w_ref[...]).astype(o_ref.dtype)


def matmul_bf16(a, w):
    m, kdim = a.shape
    n = w.shape[1]
    tm = min(TM_MM, m)
    tn = min(512, n)
    return pl.pallas_call(
        _mm_kernel,
        grid=(m // tm, n // tn),
        in_specs=[pl.BlockSpec((tm, kdim), lambda i, j: (i, 0)),
                  pl.BlockSpec((kdim, tn), lambda i, j: (0, j))],
        out_specs=pl.BlockSpec((tm, tn), lambda i, j: (i, j)),
        out_shape=jax.ShapeDtypeStruct((m, n), BF16),
        compiler_params=_cparams(("arbitrary", "arbitrary")),
        name="matmul",
    )(a, w)


def _bias_kernel(rb_ref, bucket_ref, o_ref):
    h = pl.program_id(0)
    bucket = bucket_ref[...]
    acc = jnp.zeros(bucket.shape, F32)
    for b in range(REL_BUCKETS):
        acc = jnp.where(bucket == b, rb_ref[b, h], acc)
    o_ref[...] = acc


def rel_bias_table(rel_bias, bucket):
    return pl.pallas_call(
        _bias_kernel,
        grid=(N_Q_HEADS,),
        in_specs=[pl.BlockSpec(memory_space=pltpu.SMEM),
                  pl.BlockSpec((BLOCK, 2 * BLOCK), lambda h: (0, 0))],
        out_specs=pl.BlockSpec((None, BLOCK, 2 * BLOCK), lambda h: (h, 0, 0)),
        out_shape=jax.ShapeDtypeStruct((N_Q_HEADS, BLOCK, 2 * BLOCK), F32),
        compiler_params=_cparams(("arbitrary",)),
        name="rel_bias_table",
    )(rel_bias, bucket)


def _attn_kernel(sink_ref, q_ref, kp_ref, kc_ref, vp_ref, vc_ref, bias_ref, o_ref):
    n = pl.program_id(1)
    qi = lax.broadcasted_iota(jnp.int32, (BLOCK, 2 * BLOCK), 0)
    kj = lax.broadcasted_iota(jnp.int32, (BLOCK, 2 * BLOCK), 1)
    dist = qi + BLOCK - kj
    valid = (dist >= 0) & (dist < WINDOW) & ((kj >= BLOCK) | (n > 0))
    k = jnp.concatenate([kp_ref[...], kc_ref[...]], axis=0)
    v = jnp.concatenate([vp_ref[...], vc_ref[...]], axis=0)
    scale = HEAD_DIM ** -0.5
    for h in range(N_Q_HEADS):
        g = h // Q_PER_KV
        qh = q_ref[:, h * HEAD_DIM:(h + 1) * HEAD_DIM]
        kg = k[:, g * HEAD_DIM:(g + 1) * HEAD_DIM]
        vg = v[:, g * HEAD_DIM:(g + 1) * HEAD_DIM]
        s = lax.dot_general(qh, kg, (((1,), (1,)), ((), ())), preferred_element_type=F32) * scale
        s = jnp.where(valid, s + bias_ref[h], NEG)
        sk = sink_ref[h]
        mx = jnp.maximum(jnp.max(s, axis=-1, keepdims=True), sk)
        p = jnp.exp(s - mx)
        den = jnp.sum(p, axis=-1, keepdims=True) + jnp.exp(sk - mx)
        o = _dot(p.astype(BF16), vg) / den
        o_ref[:, h * HEAD_DIM:(h + 1) * HEAD_DIM] = o.astype(o_ref.dtype)


def swa_attention(z, bias_tab, sinks):
    m = z.shape[0]
    nb = SEQ // BLOCK
    kcol = Q_WIDTH // KV_WIDTH
    return pl.pallas_call(
        _attn_kernel,
        grid=(BATCH, nb),
        in_specs=[pl.BlockSpec(memory_space=pltpu.SMEM),
                  pl.BlockSpec((BLOCK, Q_WIDTH), lambda b, n: (b * nb + n, 0)),
                  pl.BlockSpec((BLOCK, KV_WIDTH), lambda b, n: (b * nb + jnp.maximum(n - 1, 0), kcol)),
                  pl.BlockSpec((BLOCK, KV_WIDTH), lambda b, n: (b * nb + n, kcol)),
                  pl.BlockSpec((BLOCK, KV_WIDTH), lambda b, n: (b * nb + jnp.maximum(n - 1, 0), kcol + 1)),
                  pl.BlockSpec((BLOCK, KV_WIDTH), lambda b, n: (b * nb + n, kcol + 1)),
                  pl.BlockSpec((N_Q_HEADS, BLOCK, 2 * BLOCK), lambda b, n: (0, 0, 0))],
        out_specs=pl.BlockSpec((BLOCK, Q_WIDTH), lambda b, n: (b * nb + n, 0)),
        out_shape=jax.ShapeDtypeStruct((m, ODD_OUT), BF16),
        compiler_params=_cparams(("arbitrary", "arbitrary")),
        name="swa_attention",
    )(sinks, z, z, z, z, z, bias_tab)


def _ssm_prep_kernel(are_l_ref, aim_l_ref, ls_l_ref, are_s_ref, aim_s_ref, ls_s_ref,
                     btr_ref, bti_ref, ctr_ref, cti_ref,
                     kt_ref, ftr_ref, fti_ref, etr_ref, eti_ref, lr_ref, li_ref, *, scan_steps):
    are = are_l_ref[...]
    aim = aim_l_ref[...]
    dt = jnp.exp(ls_l_ref[...])
    lre = are * dt
    th = aim * dt
    mag = jnp.exp(lre)
    lam_r = mag * jnp.cos(th)
    lam_i = mag * jnp.sin(th)
    den = are * are + aim * aim
    n_r = lam_r - 1.0
    n_i = lam_i
    f_r = (n_r * are + n_i * aim) / den
    f_i = (n_i * are - n_r * aim) / den
    btr = btr_ref[...]
    bti = bti_ref[...]
    bb_r = f_r * btr - f_i * bti
    bb_i = f_r * bti + f_i * btr
    kk = (CHUNK - 1 - lax.broadcasted_iota(jnp.int32, (CHUNK, SSM_STATE), 0)).astype(F32)
    pmag = jnp.exp(kk * lre)
    pw_r = pmag * jnp.cos(kk * th)
    pw_i = pmag * jnp.sin(kk * th)
    for s in range(CHUNK):
        pr = pw_r[s:s + 1, :]
        pi = pw_i[s:s + 1, :]
        ftr_ref[s * SSM_GROUP_CH:(s + 1) * SSM_GROUP_CH, :] = bb_r * pr - bb_i * pi
        fti_ref[s * SSM_GROUP_CH:(s + 1) * SSM_GROUP_CH, :] = bb_r * pi + bb_i * pr
    cmag = jnp.exp(float(CHUNK) * lre)
    cr = cmag * jnp.cos(float(CHUNK) * th)
    ci = cmag * jnp.sin(float(CHUNK) * th)
    for k in range(scan_steps):
        lr_ref[k:k + 1, :] = cr
        li_ref[k:k + 1, :] = ci
        cr, ci = cr * cr - ci * ci, 2.0 * cr * ci
    are_s = are_s_ref[...]
    aim_s = aim_s_ref[...]
    dt_s = jnp.exp(ls_s_ref[...])
    lre_s = are_s * dt_s
    th_s = aim_s * dt_s
    kl = (lax.broadcasted_iota(jnp.int32, (1, CHUNK_W), 1) // SSM_GROUP_CH).astype(F32)
    ctr = ctr_ref[...]
    cti = cti_ref[...]

    def c_times_power(k0):
        e = kl + k0
        mg = jnp.exp(e * lre_s)
        pr = mg * jnp.cos(e * th_s)
        pi = mg * jnp.sin(e * th_s)
        return ctr * pr - cti * pi, ctr * pi + cti * pr

    w0_r, w0_i = c_times_power(0.0)
    hi = lax.Precision.HIGHEST
    kt_ref[...] = (jnp.dot(bb_r, w0_r, precision=hi, preferred_element_type=F32)
                   - jnp.dot(bb_i, w0_i, precision=hi, preferred_element_type=F32))
    w1_r, w1_i = c_times_power(1.0)
    etr_ref[...] = w1_r
    eti_ref[...] = -w1_i


def ssm_prep(a_re, a_im, log_step, b_re, b_im, c_re, c_im, scan_steps):
    g, p, h = SSM_GROUPS, SSM_STATE, SSM_GROUP_CH
    ls = log_step.astype(F32)
    lane = lambda a: a.astype(F32).reshape(g, 1, p)
    sub = lambda a: a.astype(F32).reshape(g, p, 1)
    ls_l = jnp.broadcast_to(ls[:, None, None], (g, 1, p))
    ls_s = jnp.broadcast_to(ls[:, None, None], (g, p, 1))
    bt = lambda b: jnp.swapaxes(b.astype(F32), 1, 2)
    ct = lambda c: jnp.tile(jnp.swapaxes(c.astype(F32), 1, 2), (1, 1, CHUNK))
    spec3 = lambda s1, s2: pl.BlockSpec((None, s1, s2), lambda i: (i, 0, 0))
    outs = pl.pallas_call(
        functools.partial(_ssm_prep_kernel, scan_steps=scan_steps),
        grid=(g,),
        in_specs=[spec3(1, p), spec3(1, p), spec3(1, p), spec3(p, 1), spec3(p, 1), spec3(p, 1),
                  spec3(h, p), spec3(h, p), spec3(p, CHUNK_W), spec3(p, CHUNK_W)],
        out_specs=[spec3(h, CHUNK_W), spec3(CHUNK_W, p), spec3(CHUNK_W, p),
                   spec3(p, CHUNK_W), spec3(p, CHUNK_W), spec3(scan_steps, p), spec3(scan_steps, p)],
        out_shape=[jax.ShapeDtypeStruct((g, h, CHUNK_W), F32),
                   jax.ShapeDtypeStruct((g, CHUNK_W, p), F32),
                   jax.ShapeDtypeStruct((g, CHUNK_W, p), F32),
                   jax.ShapeDtypeStruct((g, p, CHUNK_W), F32),
                   jax.ShapeDtypeStruct((g, p, CHUNK_W), F32),
                   jax.ShapeDtypeStruct((g, scan_steps, p), F32),
                   jax.ShapeDtypeStruct((g, scan_steps, p), F32)],
        compiler_params=_cparams(("arbitrary",)),
        name="ssm_prep",
    )(lane(a_re), lane(a_im), ls_l, sub(a_re), sub(a_im), ls_s,
      bt(b_re), bt(b_im), ct(c_re), ct(c_im))
    kt, ftr, fti, etr, eti, lr, li = outs
    ktp = jnp.pad(kt, ((0, 0), (0, 0), ((CHUNK - 1) * h, 0)))
    rows = [ktp[:, :, (CHUNK - 1 - s) * h:(CHUNK - 1 - s) * h + CHUNK_W] for s in range(CHUNK)]
    mmat = jnp.stack(rows, axis=1).reshape(g, CHUNK_W, CHUNK_W)
    ft = jnp.concatenate([ftr, fti], axis=2)
    et = jnp.concatenate([etr, eti], axis=1)
    l_same = jnp.concatenate([lr, lr], axis=2)
    l_swap = jnp.concatenate([-li, li], axis=2)
    return mmat.astype(BF16), ft.astype(BF16), et.astype(BF16), l_same, l_swap


def _ssm_kernel(u_ref, m_ref, ft_ref, et_ref, ls_ref, lw_ref, d_ref, y_ref,
                *, gb, n_chunks, scan_steps):
    rows = u_ref.shape[1]
    ridx = lax.broadcasted_iota(jnp.int32, (rows, 1), 0) % n_chunks
    for gi in range(gb):
        u = u_ref[gi]
        x = _dot(u, ft_ref[gi])
        for k in range(scan_steps):
            d = 1 << k
            sh = jnp.where(ridx >= d, pltpu.roll(x, d, 0), 0.0)
            x = x + ls_ref[gi, k:k + 1, :] * sh + lw_ref[gi, k:k + 1, :] * pltpu.roll(sh, SSM_STATE, 1)
        prev = jnp.where(ridx >= 1, pltpu.roll(x, 1, 0), 0.0)
        y = _dot(u, m_ref[gi]) + _dot(prev.astype(BF16), et_ref[gi]) + d_ref[gi] * u.astype(F32)
        y_ref[gi] = y


def ssm_chunked(u_flat, mmat, ft, et, l_same, l_swap, d_flat, n_chunks, scan_steps):
    g, rows, _ = u_flat.shape
    gb = SSM_GROUP_BLOCK
    p2 = 2 * SSM_STATE
    spec = lambda s1, s2: pl.BlockSpec((gb, s1, s2), lambda i: (i, 0, 0))
    return pl.pallas_call(
        functools.partial(_ssm_kernel, gb=gb, n_chunks=n_chunks, scan_steps=scan_steps),
        grid=(g // gb,),
        in_specs=[spec(rows, CHUNK_W), spec(CHUNK_W, CHUNK_W), spec(CHUNK_W, p2), spec(p2, CHUNK_W),
                  spec(scan_steps, p2), spec(scan_steps, p2), spec(1, CHUNK_W)],
        out_specs=spec(rows, CHUNK_W),
        out_shape=jax.ShapeDtypeStruct((g, rows, CHUNK_W), F32),
        compiler_params=_cparams(("arbitrary",)),
        name="ssm_chunked",
    )(u_flat, mmat, ft, et, l_same, l_swap, d_flat)


def _glu_kernel(y_ref, w_ref, b_ref, buf_ref, o_ref):
    del buf_ref
    y = y_ref[...]
    gel = 0.5 * y * (1.0 + jnp.tanh(math.sqrt(2.0 / math.pi) * (y + 0.044715 * (y * y * y))))
    t = _dot(gel.astype(BF16), w_ref[...]) + b_ref[...]
    o_ref[...] = (gel * (1.0 / (1.0 + jnp.exp(-t)))).astype(o_ref.dtype)


def gelu_glu(y, w_glu, b_glu, y_buf):
    m = y.shape[0]
    tm = min(TM_GLU, m)
    return pl.pallas_call(
        _glu_kernel,
        grid=(m // tm,),
        in_specs=[pl.BlockSpec((tm, SSM_WIDTH), lambda i: (i, 0)),
                  pl.BlockSpec((SSM_WIDTH, SSM_WIDTH), lambda i: (0, 0)),
                  pl.BlockSpec((1, SSM_WIDTH), lambda i: (0, 0)),
                  pl.BlockSpec(memory_space=pl.ANY)],
        out_specs=pl.BlockSpec((tm, SSM_WIDTH), lambda i: (i, Q_WIDTH // SSM_WIDTH)),
        out_shape=jax.ShapeDtypeStruct(y_buf.shape, y_buf.dtype),
        input_output_aliases={3: 0},
        compiler_params=_cparams(("arbitrary",)),
        name="gelu_glu",
    )(y, w_glu, b_glu.reshape(1, SSM_WIDTH), y_buf)


def _t5_bucket_table():
    qi = jnp.arange(BLOCK)[:, None]
    kj = jnp.arange(2 * BLOCK)[None, :]
    n = jnp.maximum(qi + BLOCK - kj, 0)
    max_exact = REL_BUCKETS // 2
    nf = jnp.maximum(n, 1).astype(F32)
    large = max_exact + (jnp.log(nf / max_exact) / math.log(REL_MAX_DIST / max_exact)
                         * (REL_BUCKETS - max_exact)).astype(jnp.int32)
    large = jnp.minimum(large, REL_BUCKETS - 1)
    return jnp.where(n < max_exact, n, large).astype(jnp.int32)


def _even_mixer(h, w_in, pool_w, pool_scale, conv_w):
    y = pool_mixer(h, w_in, pool_w, pool_scale)
    return sconv_mixer(h, w_in, conv_w, y)


def _odd_mixer(h, w_in, sinks, bias_tab, a_re, a_im, log_step, b_re, b_im, c_re, c_im,
               d_skip, w_glu, b_glu):
    m_tok = h.shape[0]
    n_chunks = SEQ // CHUNK
    scan_steps = max(1, (n_chunks - 1).bit_length())
    rows = m_tok // CHUNK
    z = matmul_bf16(h, w_in)
    y_buf = swa_attention(z, bias_tab, sinks)
    mmat, ft, et, l_same, l_swap = ssm_prep(a_re, a_im, log_step, b_re, b_im, c_re, c_im, scan_steps)
    u = z[:, Q_WIDTH + 2 * KV_WIDTH:]
    u_flat = (u.reshape(rows, CHUNK, SSM_GROUPS, SSM_GROUP_CH)
              .transpose(2, 0, 1, 3).reshape(SSM_GROUPS, rows, CHUNK_W))
    d_flat = jnp.tile(d_skip.astype(F32).reshape(SSM_GROUPS, 1, SSM_GROUP_CH), (1, 1, CHUNK))
    y_flat = ssm_chunked(u_flat, mmat, ft, et, l_same, l_swap, d_flat, n_chunks, scan_steps)
    y = (y_flat.reshape(SSM_GROUPS, rows, CHUNK, SSM_GROUP_CH)
         .transpose(1, 2, 0, 3).reshape(m_tok, SSM_WIDTH))
    return gelu_glu(y, w_glu, b_glu, y_buf)


def kernel(x, rel_bias, norm_mix_pre, norm_mix_post, norm_ffn_pre, norm_ffn_post, e_w_in, e_pool_w, e_pool_scale, e_conv_w, e_w_out, o_w_in, o_sinks, o_a_re, o_a_im, o_log_step, o_b_re, o_b_im, o_c_re, o_c_im, o_d, o_glu_w, o_glu_b, o_w_out, f_w_up, f_conv_w, f_w_down):
    bf = lambda a: a.astype(BF16)
    xr = x.reshape(BATCH * SEQ, D_MODEL).astype(F32)
    bias_tab = rel_bias_table(rel_bias.astype(F32), _t5_bucket_table())
    h = rms_norm_bf16(xr, norm_mix_pre[0].astype(F32))
    for i in range(DEPTH):
        j = i // 2
        if i % 2 == 0:
            y = _even_mixer(h, bf(e_w_in[j]), bf(e_pool_w[j]), e_pool_scale[j].astype(F32),
                            e_conv_w[j].astype(F32))
            w_out = bf(e_w_out[j])
        else:
            y = _odd_mixer(h, bf(o_w_in[j]), o_sinks[j].astype(F32), bias_tab,
                           o_a_re[j], o_a_im[j], o_log_step[j], o_b_re[j], o_b_im[j],
                           o_c_re[j], o_c_im[j], o_d[j], bf(o_glu_w[j]), o_glu_b[j].astype(F32))
            w_out = bf(o_w_out[j])
        xr, h = outproj_residual_norm(y, w_out, xr, norm_mix_post[i].astype(F32),
                                      norm_ffn_pre[i].astype(F32))
        act = ffn_up(h, bf(f_w_up[i]), f_conv_w[i].astype(F32))
        g_next = norm_mix_pre[i + 1].astype(F32) if i + 1 < DEPTH else None
        xr, h = outproj_residual_norm(act, bf(f_w_down[i]), xr, norm_ffn_post[i].astype(F32), g_next)
    return xr.reshape(x.shape)
```

```python
import functools
import math

import jax
import jax.numpy as jnp
from jax import lax
from jax.experimental import pallas as pl
from jax.experimental.pallas import tpu as pltpu

F32 = jnp.float32
BF16 = jnp.bfloat16

D_MODEL = 4096
BATCH = 4
SEQ = 4096
DEPTH = 4

POOL_WIDTH = D_MODEL // 2
POOL_WINDOWS = (2, 4, 8, 16)
POOL_GROUP = POOL_WIDTH // len(POOL_WINDOWS)
SCONV_WIDTH = D_MODEL // 2
HEAD_DIM = 64
N_Q_HEADS = D_MODEL // 2 // HEAD_DIM
N_KV_HEADS = N_Q_HEADS // 4
Q_PER_KV = N_Q_HEADS // N_KV_HEADS
WINDOW = 128
BLOCK = 128
Q_WIDTH = N_Q_HEADS * HEAD_DIM
KV_WIDTH = N_KV_HEADS * HEAD_DIM
SSM_WIDTH = D_MODEL // 4
SSM_GROUP_CH = 16
SSM_GROUPS = SSM_WIDTH // SSM_GROUP_CH
SSM_STATE = 64
ODD_OUT = Q_WIDTH + SSM_WIDTH
REL_BUCKETS = 32
REL_MAX_DIST = 128
D_FF = ((8 * D_MODEL // 3 + 255) // 256) * 256
EPS = 1e-6

CHUNK = 16
CHUNK_W = CHUNK * SSM_GROUP_CH

HALO = 16
CONV_HALO = 8
NEG = -1e30

TM_NORM = 256
TM_MIX = 512
TM_OUT = 512
TM_FFN = 1024
FFN_ROW_CHUNKS = 2
TM_MM = 1024
TM_GLU = 512
TN_FFN = 256
SSM_GROUP_BLOCK = 4

VMEM_LIMIT = 56 * 1024 * 1024


def _cparams(sem):
    return pltpu.CompilerParams(dimension_semantics=sem, vmem_limit_bytes=VMEM_LIMIT)


def _dot(a, b):
    return jnp.dot(a, b, preferred_element_type=F32)


def _load_history(ext_view, carry_view, is_first, rows):
    @pl.when(is_first)
    def _():
        ext_view[0:rows, :] = jnp.zeros((rows, ext_view.shape[1]), F32)

    @pl.when(jnp.logical_not(is_first))
    def _():
        ext_view[0:rows, :] = carry_view[...]


def _rms_kernel(x_ref, g_ref, o_ref):
    x = x_ref[...]
    r = lax.rsqrt(jnp.mean(x * x, axis=-1, keepdims=True) + EPS)
    o_ref[...] = (x * r * g_ref[...]).astype(o_ref.dtype)


def rms_norm_bf16(x, g):
    m, d = x.shape
    tm = min(TM_NORM, m)
    return pl.pallas_call(
        _rms_kernel,
        grid=(m // tm,),
        in_specs=[pl.BlockSpec((tm, d), lambda i: (i, 0)),
                  pl.BlockSpec((1, d), lambda i: (0, 0))],
        out_specs=pl.BlockSpec((tm, d), lambda i: (i, 0)),
        out_shape=jax.ShapeDtypeStruct((m, d), BF16),
        compiler_params=_cparams(("arbitrary",)),
        name="rms_norm",
    )(x, g.reshape(1, d))


def _pool_kernel(h_ref, w_ref, pw_ref, sc_ref, o_ref, carry_ref, ext_ref, *, tm, tiles_per_seq):
    i = pl.program_id(0)
    j = pl.program_id(1)
    seq_tile = i % tiles_per_seq
    _load_history(ext_ref, carry_ref.at[j], seq_tile == 0, HALO)
    a = _dot(h_ref[...], w_ref[...])
    ext_ref[HALO:, :] = a
    carry_ref[j] = a[tm - HALO:, :]
    t = seq_tile * tm + lax.broadcasted_iota(jnp.int32, (tm, 1), 0)
    for gi, w in enumerate(POOL_WINDOWS):
        @pl.when(j == gi)
        def _():
            cur = ext_ref[HALO:, :]
            s = cur
            for d in range(1, w):
                s = s + ext_ref[HALO - d:HALO - d + tm, :]
            den = jnp.minimum(t + 1, w).astype(F32)
            p = s / den - cur
            y = _dot(p.astype(BF16), pw_ref[...]) * sc_ref[...]
            o_ref[...] = y.astype(o_ref.dtype)


def pool_mixer(h, w_in, pool_w, pool_scale):
    m, d = h.shape
    tn = POOL_GROUP
    tm = min(TM_MIX, SEQ)
    ng = len(POOL_WINDOWS)
    return pl.pallas_call(
        functools.partial(_pool_kernel, tm=tm, tiles_per_seq=SEQ // tm),
        grid=(m // tm, ng),
        in_specs=[pl.BlockSpec((tm, d), lambda i, j: (i, 0)),
                  pl.BlockSpec((d, tn), lambda i, j: (0, j)),
                  pl.BlockSpec((None, tn, tn), lambda i, j: (j, 0, 0)),
                  pl.BlockSpec((1, tn), lambda i, j: (0, j))],
        out_specs=pl.BlockSpec((tm, tn), lambda i, j: (i, j)),
        out_shape=jax.ShapeDtypeStruct((m, POOL_WIDTH + SCONV_WIDTH), BF16),
        scratch_shapes=[pltpu.VMEM((ng, HALO, tn), F32),
                        pltpu.VMEM((tm + HALO, tn), F32)],
        compiler_params=_cparams(("arbitrary", "arbitrary")),
        name="pool_mixer",
    )(h, w_in, pool_w, pool_scale.reshape(1, POOL_WIDTH))


def _sconv_kernel(h_ref, whb_ref, wgb_ref, wgc_ref, cw_ref, y_in_ref, o_ref, carry_ref, ext_ref,
                  *, tm, tiles_per_seq):
    del y_in_ref
    i = pl.program_id(0)
    j = pl.program_id(1)
    _load_history(ext_ref, carry_ref.at[j], i % tiles_per_seq == 0, CONV_HALO)
    h = h_ref[...]
    u = _dot(h, wgc_ref[...]) * _dot(h, whb_ref[...])
    ext_ref[CONV_HALO:, :] = u
    carry_ref[j] = u[tm - CONV_HALO:, :]
    cw = cw_ref[...]
    conv = (cw[2:3, :] * ext_ref[CONV_HALO:, :]
            + cw[1:2, :] * ext_ref[CONV_HALO - 1:CONV_HALO - 1 + tm, :]
            + cw[0:1, :] * ext_ref[CONV_HALO - 2:CONV_HALO - 2 + tm, :])
    o_ref[...] = (_dot(h, wgb_ref[...]) * conv).astype(o_ref.dtype)


def sconv_mixer(h, w_in, conv_w, y_buf):
    m, d = h.shape
    tn = min(512, SCONV_WIDTH)
    tm = min(TM_MIX, SEQ)
    nb = SCONV_WIDTH // tn
    off = POOL_WIDTH // tn
    return pl.pallas_call(
        functools.partial(_sconv_kernel, tm=tm, tiles_per_seq=SEQ // tm),
        grid=(m // tm, nb),
        in_specs=[pl.BlockSpec((tm, d), lambda i, j: (i, 0)),
                  pl.BlockSpec((d, tn), lambda i, j: (0, off + j)),
                  pl.BlockSpec((d, tn), lambda i, j: (0, off + nb + j)),
                  pl.BlockSpec((d, tn), lambda i, j: (0, off + 2 * nb + j)),
                  pl.BlockSpec((3, tn), lambda i, j: (0, j)),
                  pl.BlockSpec(memory_space=pl.ANY)],
        out_specs=pl.BlockSpec((tm, tn), lambda i, j: (i, off + j)),
        out_shape=jax.ShapeDtypeStruct(y_buf.shape, y_buf.dtype),
        input_output_aliases={5: 0},
        scratch_shapes=[pltpu.VMEM((nb, CONV_HALO, tn), F32),
                        pltpu.VMEM((tm + CONV_HALO, tn), F32)],
        compiler_params=_cparams(("arbitrary", "arbitrary")),
        name="sconv_mixer",
    )(h, w_in, w_in, w_in, conv_w, y_buf)


def _outproj_kernel(y_ref, w_ref, x_ref, gpost_ref, gpre_ref, xo_ref, *rest, nk, emit_h):
    ho_ref = rest[0] if emit_h else None
    k = pl.program_id(1)

    @pl.when(k == 0)
    def _():
        xo_ref[...] = jnp.zeros_like(xo_ref)

    xo_ref[...] += _dot(y_ref[...], w_ref[...])

    @pl.when(k == nk - 1)
    def _():
        mix = xo_ref[...]
        r = lax.rsqrt(jnp.mean(mix * mix, axis=-1, keepdims=True) + EPS)
        xn = x_ref[...] + mix * r * gpost_ref[...]
        xo_ref[...] = xn
        if emit_h:
            r2 = lax.rsqrt(jnp.mean(xn * xn, axis=-1, keepdims=True) + EPS)
            ho_ref[...] = (xn * r2 * gpre_ref[...]).astype(ho_ref.dtype)


def outproj_residual_norm(y, w, x, g_post, g_pre_next):
    m, kdim = y.shape
    d = w.shape[1]
    tm = min(TM_OUT, m)
    tk = 512 if kdim % 512 == 0 else 256
    nk = kdim // tk
    emit_h = g_pre_next is not None
    if not emit_h:
        g_pre_next = g_post
    out_shape = [jax.ShapeDtypeStruct((m, d), F32)]
    out_specs = [pl.BlockSpec((tm, d), lambda i, k: (i, 0))]
    if emit_h:
        out_shape.append(jax.ShapeDtypeStruct((m, d), BF16))
        out_specs.append(pl.BlockSpec((tm, d), lambda i, k: (i, 0)))
    res = pl.pallas_call(
        functools.partial(_outproj_kernel, nk=nk, emit_h=emit_h),
        grid=(m // tm, nk),
        in_specs=[pl.BlockSpec((tm, tk), lambda i, k: (i, k)),
                  pl.BlockSpec((tk, d), lambda i, k: (k, 0)),
                  pl.BlockSpec((tm, d), lambda i, k: (i, 0), pipeline_mode=pl.Buffered(1)),
                  pl.BlockSpec((1, d), lambda i, k: (0, 0)),
                  pl.BlockSpec((1, d), lambda i, k: (0, 0))],
        out_specs=out_specs,
        out_shape=out_shape,
        compiler_params=_cparams(("arbitrary", "arbitrary")),
        name="outproj_norm",
    )(y, w, x, g_post.reshape(1, d), g_pre_next.reshape(1, d))
    return (res[0], res[1]) if emit_h else (res[0], None)


def _ffn_pad():
    return -(-D_FF // 512) * 512


def ffn_interleave(a):
    rows = a.shape[0]
    ffp = _ffn_pad()
    a = jnp.pad(a.reshape(rows, 2, D_FF), ((0, 0), (0, 0), (0, ffp - D_FF)))
    return a.reshape(rows, 2, ffp // TN_FFN, TN_FFN).transpose(0, 2, 1, 3).reshape(rows, 2 * ffp)


def _ffn_up_kernel(h_ref, w_ref, cw_ref, o_ref, carry_ref, ext_ref, *, tm, tn, row_chunks, tiles_per_seq):
    i = pl.program_id(0)
    j = pl.program_id(1)
    _load_history(ext_ref, carry_ref.at[j], i % tiles_per_seq == 0, CONV_HALO)
    cw = cw_ref[...]
    rows = tm // row_chunks
    for c in range(row_chunks):
        lo = CONV_HALO + c * rows
        z = _dot(h_ref[c * rows:(c + 1) * rows, :], w_ref[...])
        ext_ref[lo:lo + rows, :] = z
        conv = (cw[2:3, :] * z
                + cw[1:2, :] * ext_ref[lo - 1:lo - 1 + rows, :]
                + cw[0:1, :] * ext_ref[lo - 2:lo - 2 + rows, :])
        gate = conv[:, :tn]
        up = conv[:, tn:]
        o_ref[c * rows:(c + 1) * rows, :] = (gate * (1.0 / (1.0 + jnp.exp(-gate))) * up).astype(o_ref.dtype)
    carry_ref[j] = ext_ref[tm:tm + CONV_HALO, :]


def ffn_up(h, w_up, conv_w):
    m, d = h.shape
    tn = TN_FFN
    tm = min(TM_FFN, SEQ)
    ffp = _ffn_pad()
    nj = ffp // tn
    return pl.pallas_call(
        functools.partial(_ffn_up_kernel, tm=tm, tn=tn, row_chunks=FFN_ROW_CHUNKS, tiles_per_seq=SEQ // tm),
        grid=(m // tm, nj),
        in_specs=[pl.BlockSpec((tm, d), lambda i, j: (i, 0)),
                  pl.BlockSpec((d, 2 * tn), lambda i, j: (0, j)),
                  pl.BlockSpec((3, 2 * tn), lambda i, j: (0, j))],
        out_specs=pl.BlockSpec((tm, tn), lambda i, j: (i, j)),
        out_shape=jax.ShapeDtypeStruct((m, ffp), BF16),
        scratch_shapes=[pltpu.VMEM((nj, CONV_HALO, 2 * tn), F32),
                        pltpu.VMEM((tm + CONV_HALO, 2 * tn), F32)],
        compiler_params=_cparams(("arbitrary", "arbitrary")),
        name="ffn_up",
    )(h, w_up, conv_w)


def _mm_kernel(a_ref, w_ref, o_ref):
    o_ref[...] = _dot(a_ref[...], w_ref[...]).astype(o_ref.dtype)


def matmul_bf16(a, w):
    m, kdim = a.shape
    n = w.shape[1]
    tm = min(TM_MM, m)
    tn = min(512, n)
    return pl.pallas_call(
        _mm_kernel,
        grid=(m // tm, n // tn),
        in_specs=[pl.BlockSpec((tm, kdim), lambda i, j: (i, 0)),
                  pl.BlockSpec((kdim, tn), lambda i, j: (0, j))],
        out_specs=pl.BlockSpec((tm, tn), lambda i, j: (i, j)),
        out_shape=jax.ShapeDtypeStruct((m, n), BF16),
        compiler_params=_cparams(("arbitrary", "arbitrary")),
        name="matmul",
    )(a, w)


def _bias_kernel(rb_ref, bucket_ref, o_ref):
    variant = pl.program_id(0)
    h = pl.program_id(1)
    bucket = bucket_ref[...]
    acc = jnp.zeros(bucket.shape, F32)
    for b in range(REL_BUCKETS):
        acc = jnp.where(bucket == b, rb_ref[b, h], acc)
    qi = lax.broadcasted_iota(jnp.int32, bucket.shape, 0)
    kj = lax.broadcasted_iota(jnp.int32, bucket.shape, 1)
    dist = qi + BLOCK - kj
    valid = (dist >= 0) & (dist < WINDOW) & ((kj >= BLOCK) | (variant > 0))
    o_ref[...] = jnp.where(valid, acc, NEG)


def rel_bias_table(rel_bias, bucket):
    return pl.pallas_call(
        _bias_kernel,
        grid=(2, N_Q_HEADS),
        in_specs=[pl.BlockSpec(memory_space=pltpu.SMEM),
                  pl.BlockSpec((BLOCK, 2 * BLOCK), lambda v, h: (0, 0))],
        out_specs=pl.BlockSpec((None, None, BLOCK, 2 * BLOCK), lambda v, h: (v, h, 0, 0)),
        out_shape=jax.ShapeDtypeStruct((2, N_Q_HEADS, BLOCK, 2 * BLOCK), F32),
        compiler_params=_cparams(("arbitrary", "arbitrary")),
        name="rel_bias_table",
    )(rel_bias, bucket)


def _attn_kernel(sink_ref, q_ref, kp_ref, kc_ref, vp_ref, vc_ref, bias_ref, o_ref):
    k = jnp.concatenate([kp_ref[...], kc_ref[...]], axis=0)
    v = jnp.concatenate([vp_ref[...], vc_ref[...]], axis=0)
    scale = HEAD_DIM ** -0.5
    for h in range(N_Q_HEADS):
        g = h // Q_PER_KV
        q = q_ref[:, h * HEAD_DIM:(h + 1) * HEAD_DIM] * scale
        kg = k[:, g * HEAD_DIM:(g + 1) * HEAD_DIM]
        vg = v[:, g * HEAD_DIM:(g + 1) * HEAD_DIM]
        s = lax.dot_general(q, kg, (((1,), (1,)), ((), ())), preferred_element_type=F32) + bias_ref[h]
        sk = sink_ref[h]
        mx = jnp.maximum(jnp.max(s, axis=-1, keepdims=True), sk)
        p = jnp.exp(s - mx)
        den = jnp.sum(p, axis=-1, keepdims=True) + jnp.exp(sk - mx)
        o = _dot(p.astype(BF16), vg) / den
        o_ref[:, h * HEAD_DIM:(h + 1) * HEAD_DIM] = o.astype(o_ref.dtype)


def swa_attention(z, bias_tab, sinks):
    m = z.shape[0]
    nb = SEQ // BLOCK
    kcol = Q_WIDTH // KV_WIDTH
    return pl.pallas_call(
        _attn_kernel,
        grid=(BATCH, nb),
        in_specs=[pl.BlockSpec(memory_space=pltpu.SMEM),
                  pl.BlockSpec((BLOCK, Q_WIDTH), lambda b, n: (b * nb + n, 0)),
                  pl.BlockSpec((BLOCK, KV_WIDTH), lambda b, n: (b * nb + jnp.maximum(n - 1, 0), kcol)),
                  pl.BlockSpec((BLOCK, KV_WIDTH), lambda b, n: (b * nb + n, kcol)),
                  pl.BlockSpec((BLOCK, KV_WIDTH), lambda b, n: (b * nb + jnp.maximum(n - 1, 0), kcol + 1)),
                  pl.BlockSpec((BLOCK, KV_WIDTH), lambda b, n: (b * nb + n, kcol + 1)),
                  pl.BlockSpec((None, N_Q_HEADS, BLOCK, 2 * BLOCK),
                               lambda b, n: (jnp.minimum(n, 1), 0, 0, 0))],
        out_specs=pl.BlockSpec((BLOCK, Q_WIDTH), lambda b, n: (b * nb + n, 0)),
        out_shape=jax.ShapeDtypeStruct((m, ODD_OUT), BF16),
        compiler_params=_cparams(("arbitrary", "arbitrary")),
        name="swa_attention",
    )(sinks, z, z, z, z, z, bias_tab)


def _ssm_prep_kernel(are_l_ref, aim_l_ref, ls_l_ref, are_s_ref, aim_s_ref, ls_s_ref,
                     btr_ref, bti_ref, ctr_ref, cti_ref,
                     kt_ref, ftr_ref, fti_ref, etr_ref, eti_ref, lr_ref, li_ref, *, scan_steps):
    are = are_l_ref[...]
    aim = aim_l_ref[...]
    dt = jnp.exp(ls_l_ref[...])
    lre = are * dt
    th = aim * dt
    mag = jnp.exp(lre)
    lam_r = mag * jnp.cos(th)
    lam_i = mag * jnp.sin(th)
    den = are * are + aim * aim
    n_r = lam_r - 1.0
    n_i = lam_i
    f_r = (n_r * are + n_i * aim) / den
    f_i = (n_i * are - n_r * aim) / den
    btr = btr_ref[...]
    bti = bti_ref[...]
    bb_r = f_r * btr - f_i * bti
    bb_i = f_r * bti + f_i * btr
    kk = (CHUNK - 1 - lax.broadcasted_iota(jnp.int32, (CHUNK, SSM_STATE), 0)).astype(F32)
    pmag = jnp.exp(kk * lre)
    pw_r = pmag * jnp.cos(kk * th)
    pw_i = pmag * jnp.sin(kk * th)
    for s in range(CHUNK):
        pr = pw_r[s:s + 1, :]
        pi = pw_i[s:s + 1, :]
        ftr_ref[s * SSM_GROUP_CH:(s + 1) * SSM_GROUP_CH, :] = bb_r * pr - bb_i * pi
        fti_ref[s * SSM_GROUP_CH:(s + 1) * SSM_GROUP_CH, :] = bb_r * pi + bb_i * pr
    cmag = jnp.exp(float(CHUNK) * lre)
    cr = cmag * jnp.cos(float(CHUNK) * th)
    ci = cmag * jnp.sin(float(CHUNK) * th)
    for k in range(scan_steps):
        lr_ref[k:k + 1, :] = cr
        li_ref[k:k + 1, :] = ci
        cr, ci = cr * cr - ci * ci, 2.0 * cr * ci
    are_s = are_s_ref[...]
    aim_s = aim_s_ref[...]
    dt_s = jnp.exp(ls_s_ref[...])
    lre_s = are_s * dt_s
    th_s = aim_s * dt_s
    kl = (lax.broadcasted_iota(jnp.int32, (1, CHUNK_W), 1) // SSM_GROUP_CH).astype(F32)
    ctr = ctr_ref[...]
    cti = cti_ref[...]

    def c_times_power(k0):
        e = kl + k0
        mg = jnp.exp(e * lre_s)
        pr = mg * jnp.cos(e * th_s)
        pi = mg * jnp.sin(e * th_s)
        return ctr * pr - cti * pi, ctr * pi + cti * pr

    w0_r, w0_i = c_times_power(0.0)
    hi = lax.Precision.HIGHEST
    kt_ref[...] = (jnp.dot(bb_r, w0_r, precision=hi, preferred_element_type=F32)
                   - jnp.dot(bb_i, w0_i, precision=hi, preferred_element_type=F32))
    w1_r, w1_i = c_times_power(1.0)
    etr_ref[...] = w1_r
    eti_ref[...] = -w1_i


def ssm_prep(a_re, a_im, log_step, b_re, b_im, c_re, c_im, scan_steps):
    g, p, h = SSM_GROUPS, SSM_STATE, SSM_GROUP_CH
    ls = log_step.astype(F32)
    lane = lambda a: a.astype(F32).reshape(g, 1, p)
    sub = lambda a: a.astype(F32).reshape(g, p, 1)
    ls_l = jnp.broadcast_to(ls[:, None, None], (g, 1, p))
    ls_s = jnp.broadcast_to(ls[:, None, None], (g, p, 1))
    bt = lambda b: jnp.swapaxes(b.astype(F32), 1, 2)
    ct = lambda c: jnp.tile(jnp.swapaxes(c.astype(F32), 1, 2), (1, 1, CHUNK))
    spec3 = lambda s1, s2: pl.BlockSpec((None, s1, s2), lambda i: (i, 0, 0))
    outs = pl.pallas_call(
        functools.partial(_ssm_prep_kernel, scan_steps=scan_steps),
        grid=(g,),
        in_specs=[spec3(1, p), spec3(1, p), spec3(1, p), spec3(p, 1), spec3(p, 1), spec3(p, 1),
                  spec3(h, p), spec3(h, p), spec3(p, CHUNK_W), spec3(p, CHUNK_W)],
        out_specs=[spec3(h, CHUNK_W), spec3(CHUNK_W, p), spec3(CHUNK_W, p),
                   spec3(p, CHUNK_W), spec3(p, CHUNK_W), spec3(scan_steps, p), spec3(scan_steps, p)],
        out_shape=[jax.ShapeDtypeStruct((g, h, CHUNK_W), F32),
                   jax.ShapeDtypeStruct((g, CHUNK_W, p), F32),
                   jax.ShapeDtypeStruct((g, CHUNK_W, p), F32),
                   jax.ShapeDtypeStruct((g, p, CHUNK_W), F32),
                   jax.ShapeDtypeStruct((g, p, CHUNK_W), F32),
                   jax.ShapeDtypeStruct((g, scan_steps, p), F32),
                   jax.ShapeDtypeStruct((g, scan_steps, p), F32)],
        compiler_params=_cparams(("arbitrary",)),
        name="ssm_prep",
    )(lane(a_re), lane(a_im), ls_l, sub(a_re), sub(a_im), ls_s,
      bt(b_re), bt(b_im), ct(c_re), ct(c_im))
    kt, ftr, fti, etr, eti, lr, li = outs
    ktp = jnp.pad(kt, ((0, 0), (0, 0), ((CHUNK - 1) * h, 0)))
    rows = [ktp[:, :, (CHUNK - 1 - s) * h:(CHUNK - 1 - s) * h + CHUNK_W] for s in range(CHUNK)]
    mmat = jnp.stack(rows, axis=1).reshape(g, CHUNK_W, CHUNK_W)
    ft = jnp.concatenate([ftr, fti], axis=2)
    et = jnp.concatenate([etr, eti], axis=1)
    l_same = jnp.concatenate([lr, lr], axis=2)
    l_swap = jnp.concatenate([-li, li], axis=2)
    return mmat.astype(BF16), ft.astype(BF16), et.astype(BF16), l_same, l_swap


def _ssm_kernel(u_ref, m_ref, ft_ref, et_ref, ls_ref, lw_ref, d_ref, y_ref,
                *, gb, n_chunks, scan_steps):
    rows = u_ref.shape[1]
    ridx = lax.broadcasted_iota(jnp.int32, (rows, 1), 0) % n_chunks
    for gi in range(gb):
        u = u_ref[gi]
        x = _dot(u, ft_ref[gi])
        for k in range(scan_steps):
            d = 1 << k
            sh = jnp.where(ridx >= d, pltpu.roll(x, d, 0), 0.0)
            x = x + ls_ref[gi, k:k + 1, :] * sh + lw_ref[gi, k:k + 1, :] * pltpu.roll(sh, SSM_STATE, 1)
        prev = jnp.where(ridx >= 1, pltpu.roll(x, 1, 0), 0.0)
        y = _dot(u, m_ref[gi]) + _dot(prev.astype(BF16), et_ref[gi]) + d_ref[gi] * u.astype(F32)
        y_ref[gi] = y


def ssm_chunked(u_flat, mmat, ft, et, l_same, l_swap, d_flat, n_chunks, scan_steps):
    g, rows, _ = u_flat.shape
    gb = SSM_GROUP_BLOCK
    p2 = 2 * SSM_STATE
    spec = lambda s1, s2: pl.BlockSpec((gb, s1, s2), lambda i: (i, 0, 0))
    return pl.pallas_call(
        functools.partial(_ssm_kernel, gb=gb, n_chunks=n_chunks, scan_steps=scan_steps),
        grid=(g // gb,),
        in_specs=[spec(rows, CHUNK_W), spec(CHUNK_W, CHUNK_W), spec(CHUNK_W, p2), spec(p2, CHUNK_W),
                  spec(scan_steps, p2), spec(scan_steps, p2), spec(1, CHUNK_W)],
        out_specs=spec(rows, CHUNK_W),
        out_shape=jax.ShapeDtypeStruct((g, rows, CHUNK_W), F32),
        compiler_params=_cparams(("arbitrary",)),
        name="ssm_chunked",
    )(u_flat, mmat, ft, et, l_same, l_swap, d_flat)


def _glu_kernel(y_ref, w_ref, b_ref, buf_ref, o_ref):
    del buf_ref
    y = y_ref[...]
    gel = 0.5 * y * (1.0 + jnp.tanh(math.sqrt(2.0 / math.pi) * (y + 0.044715 * (y * y * y))))
    t = _dot(gel.astype(BF16), w_ref[...]) + b_ref[...]
    o_ref[...] = (gel * (1.0 / (1.0 + jnp.exp(-t)))).astype(o_ref.dtype)


def gelu_glu(y, w_glu, b_glu, y_buf):
    m = y.shape[0]
    tm = min(TM_GLU, m)
    return pl.pallas_call(
        _glu_kernel,
        grid=(m // tm,),
        in_specs=[pl.BlockSpec((tm, SSM_WIDTH), lambda i: (i, 0)),
                  pl.BlockSpec((SSM_WIDTH, SSM_WIDTH), lambda i: (0, 0)),
                  pl.BlockSpec((1, SSM_WIDTH), lambda i: (0, 0)),
                  pl.BlockSpec(memory_space=pl.ANY)],
        out_specs=pl.BlockSpec((tm, SSM_WIDTH), lambda i: (i, Q_WIDTH // SSM_WIDTH)),
        out_shape=jax.ShapeDtypeStruct(y_buf.shape, y_buf.dtype),
        input_output_aliases={3: 0},
        compiler_params=_cparams(("arbitrary",)),
        name="gelu_glu",
    )(y, w_glu, b_glu.reshape(1, SSM_WIDTH), y_buf)


def _t5_bucket_table():
    qi = jnp.arange(BLOCK)[:, None]
    kj = jnp.arange(2 * BLOCK)[None, :]
    n = jnp.maximum(qi + BLOCK - kj, 0)
    max_exact = REL_BUCKETS // 2
    nf = jnp.maximum(n, 1).astype(F32)
    large = max_exact + (jnp.log(nf / max_exact) / math.log(REL_MAX_DIST / max_exact)
                         * (REL_BUCKETS - max_exact)).astype(jnp.int32)
    large = jnp.minimum(large, REL_BUCKETS - 1)
    return jnp.where(n < max_exact, n, large).astype(jnp.int32)


def _even_mixer(h, w_in, pool_w, pool_scale, conv_w):
    y = pool_mixer(h, w_in, pool_w, pool_scale)
    return sconv_mixer(h, w_in, conv_w, y)


def _odd_mixer(h, w_in, sinks, bias_tab, a_re, a_im, log_step, b_re, b_im, c_re, c_im,
               d_skip, w_glu, b_glu):
    m_tok = h.shape[0]
    n_chunks = SEQ // CHUNK
    scan_steps = max(1, (n_chunks - 1).bit_length())
    rows = m_tok // CHUNK
    z = matmul_bf16(h, w_in)
    y_buf = swa_attention(z, bias_tab, sinks)
    mmat, ft, et, l_same, l_swap = ssm_prep(a_re, a_im, log_step, b_re, b_im, c_re, c_im, scan_steps)
    u = z[:, Q_WIDTH + 2 * KV_WIDTH:]
    u_flat = (u.reshape(rows, CHUNK, SSM_GROUPS, SSM_GROUP_CH)
              .transpose(2, 0, 1, 3).reshape(SSM_GROUPS, rows, CHUNK_W))
    d_flat = jnp.tile(d_skip.astype(F32).reshape(SSM_GROUPS, 1, SSM_GROUP_CH), (1, 1, CHUNK))
    y_flat = ssm_chunked(u_flat, mmat, ft, et, l_same, l_swap, d_flat, n_chunks, scan_steps)
    y = (y_flat.reshape(SSM_GROUPS, rows, CHUNK, SSM_GROUP_CH)
         .transpose(1, 2, 0, 3).reshape(m_tok, SSM_WIDTH))
    return gelu_glu(y, w_glu, b_glu, y_buf)


def kernel(x, rel_bias, norm_mix_pre, norm_mix_post, norm_ffn_pre, norm_ffn_post, e_w_in, e_pool_w, e_pool_scale, e_conv_w, e_w_out, o_w_in, o_sinks, o_a_re, o_a_im, o_log_step, o_b_re, o_b_im, o_c_re, o_c_im, o_d, o_glu_w, o_glu_b, o_w_out, f_w_up, f_conv_w, f_w_down):
    bf = lambda a: a.astype(BF16)
    xr = x.reshape(BATCH * SEQ, D_MODEL).astype(F32)
    bias_tab = rel_bias_table(rel_bias.astype(F32), _t5_bucket_table())
    h = rms_norm_bf16(xr, norm_mix_pre[0].astype(F32))
    for i in range(DEPTH):
        j = i // 2
        if i % 2 == 0:
            y = _even_mixer(h, bf(e_w_in[j]), bf(e_pool_w[j]), e_pool_scale[j].astype(F32),
                            e_conv_w[j].astype(F32))
            w_out = bf(e_w_out[j])
        else:
            y = _odd_mixer(h, bf(o_w_in[j]), o_sinks[j].astype(F32), bias_tab,
                           o_a_re[j], o_a_im[j], o_log_step[j], o_b_re[j], o_b_im[j],
                           o_c_re[j], o_c_im[j], o_d[j], bf(o_glu_w[j]), o_glu_b[j].astype(F32))
            w_out = bf(o_w_out[j])
        xr, h = outproj_residual_norm(y, w_out, xr, norm_mix_post[i].astype(F32),
                                      norm_ffn_pre[i].astype(F32))
        act = ffn_up(h, ffn_interleave(bf(f_w_up[i])), ffn_interleave(f_conv_w[i].astype(F32)))
        w_down = jnp.pad(bf(f_w_down[i]), ((0, _ffn_pad() - D_FF), (0, 0)))
        g_next = norm_mix_pre[i + 1].astype(F32) if i + 1 < DEPTH else None
        xr, h = outproj_residual_norm(act, w_down, xr, norm_ffn_post[i].astype(F32), g_next)
    return xr.reshape(x.shape)
```

```python
import functools
import math

import jax
import jax.numpy as jnp
from jax import lax
from jax.experimental import pallas as pl
from jax.experimental.pallas import tpu as pltpu

F32 = jnp.float32
BF16 = jnp.bfloat16

D_MODEL = 4096
BATCH = 4
SEQ = 4096
DEPTH = 4

POOL_WIDTH = D_MODEL // 2
POOL_WINDOWS = (2, 4, 8, 16)
POOL_GROUP = POOL_WIDTH // len(POOL_WINDOWS)
SCONV_WIDTH = D_MODEL // 2
HEAD_DIM = 64
N_Q_HEADS = D_MODEL // 2 // HEAD_DIM
N_KV_HEADS = N_Q_HEADS // 4
Q_PER_KV = N_Q_HEADS // N_KV_HEADS
WINDOW = 128
BLOCK = 128
Q_WIDTH = N_Q_HEADS * HEAD_DIM
KV_WIDTH = N_KV_HEADS * HEAD_DIM
SSM_WIDTH = D_MODEL // 4
SSM_GROUP_CH = 16
SSM_GROUPS = SSM_WIDTH // SSM_GROUP_CH
SSM_STATE = 64
ODD_OUT = Q_WIDTH + SSM_WIDTH
REL_BUCKETS = 32
REL_MAX_DIST = 128
D_FF = ((8 * D_MODEL // 3 + 255) // 256) * 256
EPS = 1e-6

CHUNK = 16
CHUNK_W = CHUNK * SSM_GROUP_CH

HALO = 16
CONV_HALO = 8
NEG = -1e30

TM_NORM = 256
TM_MIX = 512
TM_OUT = 512
TM_FFN = 1024
FFN_ROW_CHUNKS = 2
TM_MM = 1024
TM_GLU = 512
TN_FFN = 256
ATTN_LOOKAHEAD = 5
SSM_LANE_GROUPS = 8

VMEM_LIMIT = 56 * 1024 * 1024


def _cparams(sem):
    return pltpu.CompilerParams(dimension_semantics=sem, vmem_limit_bytes=VMEM_LIMIT)


def _dot(a, b):
    return jnp.dot(a, b, preferred_element_type=F32)


def _load_history(ext_view, carry_view, is_first, rows):
    @pl.when(is_first)
    def _():
        ext_view[0:rows, :] = jnp.zeros((rows, ext_view.shape[1]), F32)

    @pl.when(jnp.logical_not(is_first))
    def _():
        ext_view[0:rows, :] = carry_view[...]


def _rms_kernel(x_ref, g_ref, o_ref):
    x = x_ref[...]
    r = lax.rsqrt(jnp.mean(x * x, axis=-1, keepdims=True) + EPS)
    o_ref[...] = (x * r * g_ref[...]).astype(o_ref.dtype)


def rms_norm_bf16(x, g):
    m, d = x.shape
    tm = min(TM_NORM, m)
    return pl.pallas_call(
        _rms_kernel,
        grid=(m // tm,),
        in_specs=[pl.BlockSpec((tm, d), lambda i: (i, 0)),
                  pl.BlockSpec((1, d), lambda i: (0, 0))],
        out_specs=pl.BlockSpec((tm, d), lambda i: (i, 0)),
        out_shape=jax.ShapeDtypeStruct((m, d), BF16),
        compiler_params=_cparams(("arbitrary",)),
        name="rms_norm",
    )(x, g.reshape(1, d))


def _pool_kernel(h_ref, w_ref, pw_ref, sc_ref, o_ref, carry_ref, ext_ref, *, tm, tiles_per_seq):
    i = pl.program_id(0)
    j = pl.program_id(1)
    seq_tile = i % tiles_per_seq
    _load_history(ext_ref, carry_ref.at[j], seq_tile == 0, HALO)
    a = _dot(h_ref[...], w_ref[...])
    ext_ref[HALO:, :] = a
    carry_ref[j] = a[tm - HALO:, :]
    t = seq_tile * tm + lax.broadcasted_iota(jnp.int32, (tm, 1), 0)
    for gi, w in enumerate(POOL_WINDOWS):
        @pl.when(j == gi)
        def _():
            cur = ext_ref[HALO:, :]
            s = cur
            for d in range(1, w):
                s = s + ext_ref[HALO - d:HALO - d + tm, :]
            den = jnp.minimum(t + 1, w).astype(F32)
            p = s / den - cur
            y = _dot(p.astype(BF16), pw_ref[...]) * sc_ref[...]
            o_ref[...] = y.astype(o_ref.dtype)


def pool_mixer(h, w_in, pool_w, pool_scale):
    m, d = h.shape
    tn = POOL_GROUP
    tm = min(TM_MIX, SEQ)
    ng = len(POOL_WINDOWS)
    return pl.pallas_call(
        functools.partial(_pool_kernel, tm=tm, tiles_per_seq=SEQ // tm),
        grid=(m // tm, ng),
        in_specs=[pl.BlockSpec((tm, d), lambda i, j: (i, 0)),
                  pl.BlockSpec((d, tn), lambda i, j: (0, j)),
                  pl.BlockSpec((None, tn, tn), lambda i, j: (j, 0, 0)),
                  pl.BlockSpec((1, tn), lambda i, j: (0, j))],
        out_specs=pl.BlockSpec((tm, tn), lambda i, j: (i, j)),
        out_shape=jax.ShapeDtypeStruct((m, POOL_WIDTH), BF16),
        scratch_shapes=[pltpu.VMEM((ng, HALO, tn), F32),
                        pltpu.VMEM((tm + HALO, tn), F32)],
        compiler_params=_cparams(("arbitrary", "arbitrary")),
        name="pool_mixer",
    )(h, w_in, pool_w, pool_scale.reshape(1, POOL_WIDTH))


def _sconv_kernel(h_ref, whb_ref, wgb_ref, wgc_ref, cw_ref, o_ref, carry_ref, ext_ref,
                  *, tm, tiles_per_seq):
    i = pl.program_id(0)
    j = pl.program_id(1)
    _load_history(ext_ref, carry_ref.at[j], i % tiles_per_seq == 0, CONV_HALO)
    h = h_ref[...]
    u = _dot(h, wgc_ref[...]) * _dot(h, whb_ref[...])
    ext_ref[CONV_HALO:, :] = u
    carry_ref[j] = u[tm - CONV_HALO:, :]
    cw = cw_ref[...]
    conv = (cw[2:3, :] * ext_ref[CONV_HALO:, :]
            + cw[1:2, :] * ext_ref[CONV_HALO - 1:CONV_HALO - 1 + tm, :]
            + cw[0:1, :] * ext_ref[CONV_HALO - 2:CONV_HALO - 2 + tm, :])
    o_ref[...] = (_dot(h, wgb_ref[...]) * conv).astype(o_ref.dtype)


def sconv_mixer(h, w_in, conv_w):
    m, d = h.shape
    tn = min(512, SCONV_WIDTH)
    tm = min(TM_MIX, SEQ)
    nb = SCONV_WIDTH // tn
    off = POOL_WIDTH // tn
    return pl.pallas_call(
        functools.partial(_sconv_kernel, tm=tm, tiles_per_seq=SEQ // tm),
        grid=(m // tm, nb),
        in_specs=[pl.BlockSpec((tm, d), lambda i, j: (i, 0)),
                  pl.BlockSpec((d, tn), lambda i, j: (0, off + j)),
                  pl.BlockSpec((d, tn), lambda i, j: (0, off + nb + j)),
                  pl.BlockSpec((d, tn), lambda i, j: (0, off + 2 * nb + j)),
                  pl.BlockSpec((3, tn), lambda i, j: (0, j))],
        out_specs=pl.BlockSpec((tm, tn), lambda i, j: (i, j)),
        out_shape=jax.ShapeDtypeStruct((m, SCONV_WIDTH), BF16),
        scratch_shapes=[pltpu.VMEM((nb, CONV_HALO, tn), F32),
                        pltpu.VMEM((tm + CONV_HALO, tn), F32)],
        compiler_params=_cparams(("arbitrary", "arbitrary")),
        name="sconv_mixer",
    )(h, w_in, w_in, w_in, conv_w)


def _outproj_kernel(*refs, nks, tail_rows, emit_h):
    ns = len(nks)
    y_refs = refs[:ns]
    w_ref, x_ref, gpost_ref, gpre_ref, xo_ref = refs[ns:ns + 5]
    ho_ref = refs[ns + 5] if emit_h else None
    nk = sum(nks)
    k = pl.program_id(1)

    @pl.when(k == 0)
    def _():
        xo_ref[...] = jnp.zeros_like(xo_ref)

    def accumulate(y_ref, masked):
        w = w_ref[...]
        if masked:
            row = lax.broadcasted_iota(jnp.int32, (w.shape[0], 1), 0)
            w = jnp.where(row < tail_rows, w, jnp.zeros_like(w))
        xo_ref[...] += _dot(y_ref[...], w)

    if ns == 1 and tail_rows is None:
        accumulate(y_refs[0], False)
    else:
        k0 = 0
        for s, nk_s in enumerate(nks):
            has_tail = tail_rows is not None and s == ns - 1
            hi = k0 + nk_s - (1 if has_tail else 0)
            if hi > k0:
                pl.when((k >= k0) & (k < hi))(functools.partial(accumulate, y_refs[s], False))
            if has_tail:
                pl.when(k == nk - 1)(functools.partial(accumulate, y_refs[s], True))
            k0 += nk_s

    @pl.when(k == nk - 1)
    def _():
        mix = xo_ref[...]
        r = lax.rsqrt(jnp.mean(mix * mix, axis=-1, keepdims=True) + EPS)
        xn = x_ref[...] + mix * r * gpost_ref[...]
        xo_ref[...] = xn
        if emit_h:
            r2 = lax.rsqrt(jnp.mean(xn * xn, axis=-1, keepdims=True) + EPS)
            ho_ref[...] = (xn * r2 * gpre_ref[...]).astype(ho_ref.dtype)


def outproj_residual_norm(ys, w, x, g_post, g_pre_next):
    m = x.shape[0]
    krows, d = w.shape
    tm = min(TM_OUT, m)
    tk = 512 if all(y.shape[1] % 512 == 0 for y in ys) else 256
    nks = tuple(y.shape[1] // tk for y in ys)
    nk = sum(nks)
    tail = krows - (nk - 1) * tk
    assert 0 < tail <= tk
    emit_h = g_pre_next is not None
    if not emit_h:
        g_pre_next = g_post
    out_shape = [jax.ShapeDtypeStruct((m, d), F32)]
    out_specs = [pl.BlockSpec((tm, d), lambda i, k: (i, 0))]
    if emit_h:
        out_shape.append(jax.ShapeDtypeStruct((m, d), BF16))
        out_specs.append(pl.BlockSpec((tm, d), lambda i, k: (i, 0)))
    y_specs = []
    k0 = 0
    for nk_s in nks:
        y_specs.append(pl.BlockSpec(
            (tm, tk), functools.partial(lambda i, k, k0, n: (i, jnp.clip(k - k0, 0, n - 1)), k0=k0, n=nk_s)))
        k0 += nk_s
    res = pl.pallas_call(
        functools.partial(_outproj_kernel, nks=nks, tail_rows=None if tail == tk else tail, emit_h=emit_h),
        grid=(m // tm, nk),
        in_specs=y_specs + [
            pl.BlockSpec((tk, d), lambda i, k: (k, 0)),
            pl.BlockSpec((tm, d), lambda i, k: (i, 0), pipeline_mode=pl.Buffered(1)),
            pl.BlockSpec((1, d), lambda i, k: (0, 0)),
            pl.BlockSpec((1, d), lambda i, k: (0, 0))],
        out_specs=out_specs,
        out_shape=out_shape,
        compiler_params=_cparams(("arbitrary", "arbitrary")),
        name="outproj_norm",
    )(*ys, w, x, g_post.reshape(1, d), g_pre_next.reshape(1, d))
    return (res[0], res[1]) if emit_h else (res[0], None)


def _ffn_pad():
    return -(-D_FF // 512) * 512


def _ffn_up_kernel(h_ref, wg_ref, wu_ref, cg_ref, cu_ref, o_ref, carry_ref, ext_ref,
                   *, tm, tn, nj, row_chunks, tiles_per_seq):
    i = pl.program_id(0)
    j = pl.program_id(1)

    @pl.when(j >= nj)
    def _():
        o_ref[...] = jnp.zeros_like(o_ref)

    @pl.when(j < nj)
    def _():
        _load_history(ext_ref, carry_ref.at[j], i % tiles_per_seq == 0, CONV_HALO)
        rows = tm // row_chunks
        for c in range(row_chunks):
            lo = CONV_HALO + c * rows
            hc = h_ref[c * rows:(c + 1) * rows, :]
            halves = []
            for w_ref, cw_ref, col in ((wg_ref, cg_ref, 0), (wu_ref, cu_ref, tn)):
                z = _dot(hc, w_ref[...])
                ext_ref[lo:lo + rows, col:col + tn] = z
                cw = cw_ref[...]
                halves.append(cw[2:3, :] * z
                              + cw[1:2, :] * ext_ref[lo - 1:lo - 1 + rows, col:col + tn]
                              + cw[0:1, :] * ext_ref[lo - 2:lo - 2 + rows, col:col + tn])
            gate, up = halves
            o_ref[c * rows:(c + 1) * rows, :] = (gate * (1.0 / (1.0 + jnp.exp(-gate))) * up).astype(o_ref.dtype)
        carry_ref[j] = ext_ref[tm:tm + CONV_HALO, :]


def ffn_up(h, w_up, conv_w):
    m, d = h.shape
    tn = TN_FFN
    tm = min(TM_FFN, SEQ)
    nj = D_FF // tn
    nj_pad = _ffn_pad() // tn
    col = lambda off: (lambda i, j: (0, off + jnp.minimum(j, nj - 1)))
    return pl.pallas_call(
        functools.partial(_ffn_up_kernel, tm=tm, tn=tn, nj=nj, row_chunks=FFN_ROW_CHUNKS,
                          tiles_per_seq=SEQ // tm),
        grid=(m // tm, nj_pad),
        in_specs=[pl.BlockSpec((tm, d), lambda i, j: (i, 0)),
                  pl.BlockSpec((d, tn), col(0)),
                  pl.BlockSpec((d, tn), col(nj)),
                  pl.BlockSpec((3, tn), col(0)),
                  pl.BlockSpec((3, tn), col(nj))],
        out_specs=pl.BlockSpec((tm, tn), lambda i, j: (i, j)),
        out_shape=jax.ShapeDtypeStruct((m, nj_pad * tn), BF16),
        scratch_shapes=[pltpu.VMEM((nj, CONV_HALO, 2 * tn), F32),
                        pltpu.VMEM((tm + CONV_HALO, 2 * tn), F32)],
        compiler_params=_cparams(("arbitrary", "arbitrary")),
        name="ffn_up",
    )(h, w_up, w_up, conv_w, conv_w)


def _mm_kernel(a_ref, w_ref, o_ref):
    o_ref[...] = _dot(a_ref[...], w_ref[...]).astype(o_ref.dtype)


def matmul_bf16(a, w):
    m, kdim = a.shape
    n = w.shape[1]
    tm = min(TM_MM, m)
    tn = min(512, n)
    return pl.pallas_call(
        _mm_kernel,
        grid=(m // tm, n // tn),
        in_specs=[pl.BlockSpec((tm, kdim), lambda i, j: (i, 0)),
                  pl.BlockSpec((kdim, tn), lambda i, j: (0, j))],
        out_specs=pl.BlockSpec((tm, tn), lambda i, j: (i, j)),
        out_shape=jax.ShapeDtypeStruct((m, n), BF16),
        compiler_params=_cparams(("arbitrary", "arbitrary")),
        name="matmul",
    )(a, w)


def _bias_kernel(rb_ref, bucket_ref, o_ref):
    variant = pl.program_id(0)
    h = pl.program_id(1)
    bucket = bucket_ref[...]
    acc = jnp.zeros(bucket.shape, F32)
    for b in range(REL_BUCKETS):
        acc = jnp.where(bucket == b, rb_ref[b, h], acc)
    qi = lax.broadcasted_iota(jnp.int32, bucket.shape, 0)
    kj = lax.broadcasted_iota(jnp.int32, bucket.shape, 1)
    dist = qi + BLOCK - kj
    valid = (dist >= 0) & (dist < WINDOW) & ((kj >= BLOCK) | (variant > 0))
    o_ref[...] = jnp.where(valid, acc, NEG)


def rel_bias_table(rel_bias, bucket):
    return pl.pallas_call(
        _bias_kernel,
        grid=(2, N_Q_HEADS),
        in_specs=[pl.BlockSpec(memory_space=pltpu.SMEM),
                  pl.BlockSpec((BLOCK, 2 * BLOCK), lambda v, h: (0, 0))],
        out_specs=pl.BlockSpec((None, None, BLOCK, 2 * BLOCK), lambda v, h: (v, h, 0, 0)),
        out_shape=jax.ShapeDtypeStruct((2, N_Q_HEADS, BLOCK, 2 * BLOCK), F32),
        compiler_params=_cparams(("arbitrary", "arbitrary")),
        name="rel_bias_table",
    )(rel_bias, bucket)


def _attn_kernel(sink_ref, q_ref, kp_ref, kc_ref, vp_ref, vc_ref, bias_ref, o_ref):
    k = jnp.concatenate([kp_ref[...], kc_ref[...]], axis=0)
    v = jnp.concatenate([vp_ref[...], vc_ref[...]], axis=0)
    scale = HEAD_DIM ** -0.5
    pair = 2 * HEAD_DIM
    low = lax.broadcasted_iota(jnp.int32, (1, pair), 1) < HEAD_DIM
    zero = jnp.zeros((), BF16)
    ones = jnp.ones((2 * BLOCK, pair), BF16)

    def swap_halves(x):
        return pltpu.bitcast(pltpu.roll(pltpu.bitcast(x, jnp.int32), HEAD_DIM, 1), BF16)

    kv = {}
    for gp in range(N_KV_HEADS // 2):
        kblk = k[:, gp * pair:(gp + 1) * pair]
        vblk = v[:, gp * pair:(gp + 1) * pair]
        kswp = swap_halves(kblk)
        vswp = swap_halves(vblk)
        for e in range(2):
            k_lo, k_hi = (kblk, kswp) if e == 0 else (kswp, kblk)
            v_lo = jnp.where(low, vblk if e == 0 else vswp, zero)
            v_hi = jnp.where(low, zero, vswp if e == 0 else vblk)
            kv[2 * gp + e] = ((k_lo, v_lo), (k_hi, v_hi))

    def scores(h):
        hp, half = divmod(h, 2)
        qp = q_ref[:, hp * pair:(hp + 1) * pair] * scale
        qm = jnp.where(low, qp, zero) if half == 0 else jnp.where(low, zero, qp)
        kk = kv[h // Q_PER_KV][half][0]
        return lax.dot_general(qm, kk, (((1,), (1,)), ((), ())), preferred_element_type=F32) + bias_ref[h]

    pending = [scores(h) for h in range(ATTN_LOOKAHEAD)]
    acc = None
    for h in range(N_Q_HEADS):
        s = pending.pop(0)
        if h + ATTN_LOOKAHEAD < N_Q_HEADS:
            pending.append(scores(h + ATTN_LOOKAHEAD))
        hp, half = divmod(h, 2)
        vv = kv[h // Q_PER_KV][half][1]
        sk = sink_ref[h]
        mx = jnp.maximum(jnp.max(s, axis=-1, keepdims=True), sk)
        p = jnp.exp(s - mx).astype(BF16)
        den = _dot(p, ones) + jnp.exp(sk - mx)
        o = _dot(p, vv) / den
        if half == 0:
            acc = o
        else:
            o_ref[:, hp * pair:(hp + 1) * pair] = (acc + o).astype(o_ref.dtype)


def swa_attention(z, bias_tab, sinks):
    m = z.shape[0]
    nb = SEQ // BLOCK
    kcol = Q_WIDTH // KV_WIDTH
    return pl.pallas_call(
        _attn_kernel,
        grid=(BATCH, nb),
        in_specs=[pl.BlockSpec(memory_space=pltpu.SMEM),
                  pl.BlockSpec((BLOCK, Q_WIDTH), lambda b, n: (b * nb + n, 0)),
                  pl.BlockSpec((BLOCK, KV_WIDTH), lambda b, n: (b * nb + jnp.maximum(n - 1, 0), kcol)),
                  pl.BlockSpec((BLOCK, KV_WIDTH), lambda b, n: (b * nb + n, kcol)),
                  pl.BlockSpec((BLOCK, KV_WIDTH), lambda b, n: (b * nb + jnp.maximum(n - 1, 0), kcol + 1)),
                  pl.BlockSpec((BLOCK, KV_WIDTH), lambda b, n: (b * nb + n, kcol + 1)),
                  pl.BlockSpec((None, N_Q_HEADS, BLOCK, 2 * BLOCK),
                               lambda b, n: (jnp.minimum(n, 1), 0, 0, 0))],
        out_specs=pl.BlockSpec((BLOCK, Q_WIDTH), lambda b, n: (b * nb + n, 0)),
        out_shape=jax.ShapeDtypeStruct((m, Q_WIDTH), BF16),
        compiler_params=_cparams(("arbitrary", "arbitrary")),
        name="swa_attention",
    )(sinks, z, z, z, z, z, bias_tab)


def _ssm_prep_kernel(are_l_ref, aim_l_ref, ls_l_ref, are_s_ref, aim_s_ref, ls_s_ref,
                     btr_ref, bti_ref, ctr_ref, cti_ref,
                     kt_ref, ftr_ref, fti_ref, etr_ref, eti_ref, lr_ref, li_ref, *, scan_steps):
    are = are_l_ref[...]
    aim = aim_l_ref[...]
    dt = jnp.exp(ls_l_ref[...])
    lre = are * dt
    th = aim * dt
    mag = jnp.exp(lre)
    lam_r = mag * jnp.cos(th)
    lam_i = mag * jnp.sin(th)
    den = are * are + aim * aim
    n_r = lam_r - 1.0
    n_i = lam_i
    f_r = (n_r * are + n_i * aim) / den
    f_i = (n_i * are - n_r * aim) / den
    btr = btr_ref[...]
    bti = bti_ref[...]
    bb_r = f_r * btr - f_i * bti
    bb_i = f_r * bti + f_i * btr
    kk = (CHUNK - 1 - lax.broadcasted_iota(jnp.int32, (CHUNK, SSM_STATE), 0)).astype(F32)
    pmag = jnp.exp(kk * lre)
    pw_r = pmag * jnp.cos(kk * th)
    pw_i = pmag * jnp.sin(kk * th)
    for s in range(CHUNK):
        pr = pw_r[s:s + 1, :]
        pi = pw_i[s:s + 1, :]
        ftr_ref[s * SSM_GROUP_CH:(s + 1) * SSM_GROUP_CH, :] = bb_r * pr - bb_i * pi
        fti_ref[s * SSM_GROUP_CH:(s + 1) * SSM_GROUP_CH, :] = bb_r * pi + bb_i * pr
    cmag = jnp.exp(float(CHUNK) * lre)
    cr = cmag * jnp.cos(float(CHUNK) * th)
    ci = cmag * jnp.sin(float(CHUNK) * th)
    for k in range(scan_steps):
        lr_ref[k:k + 1, :] = cr
        li_ref[k:k + 1, :] = ci
        cr, ci = cr * cr - ci * ci, 2.0 * cr * ci
    are_s = are_s_ref[...]
    aim_s = aim_s_ref[...]
    dt_s = jnp.exp(ls_s_ref[...])
    lre_s = are_s * dt_s
    th_s = aim_s * dt_s
    kl = (lax.broadcasted_iota(jnp.int32, (1, CHUNK_W), 1) // SSM_GROUP_CH).astype(F32)
    ctr = ctr_ref[...]
    cti = cti_ref[...]

    def c_times_power(k0):
        e = kl + k0
        mg = jnp.exp(e * lre_s)
        pr = mg * jnp.cos(e * th_s)
        pi = mg * jnp.sin(e * th_s)
        return ctr * pr - cti * pi, ctr * pi + cti * pr

    w0_r, w0_i = c_times_power(0.0)
    hi = lax.Precision.HIGHEST
    kt_ref[...] = (jnp.dot(bb_r, w0_r, precision=hi, preferred_element_type=F32)
                   - jnp.dot(bb_i, w0_i, precision=hi, preferred_element_type=F32))
    w1_r, w1_i = c_times_power(1.0)
    etr_ref[...] = w1_r
    eti_ref[...] = -w1_i


def ssm_prep(a_re, a_im, log_step, b_re, b_im, c_re, c_im, scan_steps):
    g, p, h = SSM_GROUPS, SSM_STATE, SSM_GROUP_CH
    ls = log_step.astype(F32)
    lane = lambda a: a.astype(F32).reshape(g, 1, p)
    sub = lambda a: a.astype(F32).reshape(g, p, 1)
    ls_l = jnp.broadcast_to(ls[:, None, None], (g, 1, p))
    ls_s = jnp.broadcast_to(ls[:, None, None], (g, p, 1))
    bt = lambda b: jnp.swapaxes(b.astype(F32), 1, 2)
    ct = lambda c: jnp.tile(jnp.swapaxes(c.astype(F32), 1, 2), (1, 1, CHUNK))
    spec3 = lambda s1, s2: pl.BlockSpec((None, s1, s2), lambda i: (i, 0, 0))
    outs = pl.pallas_call(
        functools.partial(_ssm_prep_kernel, scan_steps=scan_steps),
        grid=(g,),
        in_specs=[spec3(1, p), spec3(1, p), spec3(1, p), spec3(p, 1), spec3(p, 1), spec3(p, 1),
                  spec3(h, p), spec3(h, p), spec3(p, CHUNK_W), spec3(p, CHUNK_W)],
        out_specs=[spec3(h, CHUNK_W), spec3(CHUNK_W, p), spec3(CHUNK_W, p),
                   spec3(p, CHUNK_W), spec3(p, CHUNK_W), spec3(scan_steps, p), spec3(scan_steps, p)],
        out_shape=[jax.ShapeDtypeStruct((g, h, CHUNK_W), F32),
                   jax.ShapeDtypeStruct((g, CHUNK_W, p), F32),
                   jax.ShapeDtypeStruct((g, CHUNK_W, p), F32),
                   jax.ShapeDtypeStruct((g, p, CHUNK_W), F32),
                   jax.ShapeDtypeStruct((g, p, CHUNK_W), F32),
                   jax.ShapeDtypeStruct((g, scan_steps, p), F32),
                   jax.ShapeDtypeStruct((g, scan_steps, p), F32)],
        compiler_params=_cparams(("arbitrary",)),
        name="ssm_prep",
    )(lane(a_re), lane(a_im), ls_l, sub(a_re), sub(a_im), ls_s,
      bt(b_re), bt(b_im), ct(c_re), ct(c_im))
    kt, ftr, fti, etr, eti, lr, li = outs
    gl = SSM_LANE_GROUPS
    nb = g // gl
    p2 = 2 * p
    same = jnp.eye(gl, dtype=bool)
    kt5 = kt.reshape(nb, gl, h, CHUNK, h)[:, :, :, ::-1, :]
    kst = jnp.where(same[None, None, :, None, :, None],
                    kt5.transpose(0, 3, 1, 2, 4)[:, :, :, :, None, :], 0.0)
    kst = kst.reshape(nb, CHUNK * gl * h, gl * h)
    ft5 = jnp.concatenate([ftr, fti], axis=2).reshape(nb, gl, CHUNK, h, p2)
    fb = jnp.where(same[None, None, :, None, :, None],
                   ft5.transpose(0, 2, 1, 3, 4)[:, :, :, :, None, :], 0.0)
    fb = fb.reshape(nb, CHUNK * gl * h, gl * p2)
    et5 = jnp.concatenate([etr, eti], axis=1).reshape(nb, gl, p2, CHUNK, h)
    eb = jnp.where(same[None, :, None, None, :, None], et5[:, :, :, :, None, :], 0.0)
    eb = eb.reshape(nb, gl * p2, CHUNK * gl * h)
    lanes = lambda a: a.reshape(nb, gl, scan_steps, p2).transpose(0, 2, 1, 3).reshape(nb, scan_steps, gl * p2)
    l_same = lanes(jnp.concatenate([lr, lr], axis=2))
    l_swap = lanes(jnp.concatenate([-li, li], axis=2))
    return kst.astype(BF16), fb.astype(BF16), eb.astype(BF16), l_same, l_swap


def _ssm_kernel(z_ref, kst_ref, fb_ref, eb_ref, ls_ref, lw_ref, d_ref, y_ref, u32_ref, s_ref,
                *, scan_steps):
    rows = z_ref.shape[0] // CHUNK
    lanes = z_ref.shape[1]
    p2 = 2 * SSM_STATE
    u32_ref[...] = z_ref[...].astype(F32)

    def tokens(t):
        return u32_ref[pl.ds(t, rows, stride=CHUNK), :]

    u_all = jnp.concatenate([tokens(t).astype(BF16) for t in range(CHUNK)], axis=1)
    s_ref[...] = _dot(u_all, fb_ref[...])
    intra = [_dot(u_all[:, :(t + 1) * lanes], kst_ref[(CHUNK - 1 - t) * lanes:, :]) for t in range(CHUNK)]
    ridx = lax.broadcasted_iota(jnp.int32, (rows, 1), 0)
    prevs = []
    for g in range(SSM_LANE_GROUPS):
        cols = slice(g * p2, (g + 1) * p2)
        x = s_ref[:, cols]
        for k in range(scan_steps):
            d = 1 << k
            sh = jnp.where(ridx >= d, pltpu.roll(x, d, 0), 0.0)
            x = x + ls_ref[k:k + 1, cols] * sh + lw_ref[k:k + 1, cols] * pltpu.roll(sh, SSM_STATE, 1)
        prevs.append(jnp.where(ridx >= 1, pltpu.roll(x, 1, 0), 0.0).astype(BF16))
    y_state = _dot(jnp.concatenate(prevs, axis=1), eb_ref[...])
    for t in range(CHUNK):
        y = y_state[:, t * lanes:(t + 1) * lanes] + intra[t] + d_ref[...] * tokens(t)
        y_ref[pl.ds(t, rows, stride=CHUNK), :] = y


def ssm_chunked(z, kst, fb, eb, l_same, l_swap, d_skip, scan_steps):
    m = z.shape[0]
    nb = kst.shape[0]
    lanes = SSM_LANE_GROUPS * SSM_GROUP_CH
    ucol = (Q_WIDTH + 2 * KV_WIDTH) // lanes
    wspec = lambda a: pl.BlockSpec((None,) + a.shape[1:], lambda i, b: (i, 0, 0))
    return pl.pallas_call(
        functools.partial(_ssm_kernel, scan_steps=scan_steps),
        grid=(nb, m // SEQ),
        in_specs=[pl.BlockSpec((SEQ, lanes), lambda i, b: (b, ucol + i)),
                  wspec(kst), wspec(fb), wspec(eb), wspec(l_same), wspec(l_swap),
                  pl.BlockSpec((1, lanes), lambda i, b: (0, i))],
        out_specs=pl.BlockSpec((SEQ, lanes), lambda i, b: (b, i)),
        out_shape=jax.ShapeDtypeStruct((m, SSM_WIDTH), F32),
        scratch_shapes=[pltpu.VMEM((SEQ, lanes), F32),
                        pltpu.VMEM((SEQ // CHUNK, SSM_LANE_GROUPS * 2 * SSM_STATE), F32)],
        compiler_params=_cparams(("arbitrary", "arbitrary")),
        name="ssm_chunked",
    )(z, kst, fb, eb, l_same, l_swap, d_skip.astype(F32).reshape(1, SSM_WIDTH))


def _glu_kernel(y_ref, w_ref, b_ref, o_ref):
    y = y_ref[...]
    gel = 0.5 * y * (1.0 + jnp.tanh(math.sqrt(2.0 / math.pi) * (y + 0.044715 * (y * y * y))))
    t = _dot(gel.astype(BF16), w_ref[...]) + b_ref[...]
    o_ref[...] = (gel * (1.0 / (1.0 + jnp.exp(-t)))).astype(o_ref.dtype)


def gelu_glu(y, w_glu, b_glu):
    m = y.shape[0]
    tm = min(TM_GLU, m)
    return pl.pallas_call(
        _glu_kernel,
        grid=(m // tm,),
        in_specs=[pl.BlockSpec((tm, SSM_WIDTH), lambda i: (i, 0)),
                  pl.BlockSpec((SSM_WIDTH, SSM_WIDTH), lambda i: (0, 0)),
                  pl.BlockSpec((1, SSM_WIDTH), lambda i: (0, 0))],
        out_specs=pl.BlockSpec((tm, SSM_WIDTH), lambda i: (i, 0)),
        out_shape=jax.ShapeDtypeStruct((m, SSM_WIDTH), BF16),
        compiler_params=_cparams(("arbitrary",)),
        name="gelu_glu",
    )(y, w_glu, b_glu.reshape(1, SSM_WIDTH))


def _t5_bucket_table():
    qi = jnp.arange(BLOCK)[:, None]
    kj = jnp.arange(2 * BLOCK)[None, :]
    n = jnp.maximum(qi + BLOCK - kj, 0)
    max_exact = REL_BUCKETS // 2
    nf = jnp.maximum(n, 1).astype(F32)
    large = max_exact + (jnp.log(nf / max_exact) / math.log(REL_MAX_DIST / max_exact)
                         * (REL_BUCKETS - max_exact)).astype(jnp.int32)
    large = jnp.minimum(large, REL_BUCKETS - 1)
    return jnp.where(n < max_exact, n, large).astype(jnp.int32)


def _even_mixer(h, w_in, pool_w, pool_scale, conv_w):
    return [pool_mixer(h, w_in, pool_w, pool_scale), sconv_mixer(h, w_in, conv_w)]


def _odd_mixer(h, w_in, sinks, bias_tab, a_re, a_im, log_step, b_re, b_im, c_re, c_im,
               d_skip, w_glu, b_glu):
    scan_steps = max(1, (SEQ // CHUNK - 1).bit_length())
    z = matmul_bf16(h, w_in)
    y_attn = swa_attention(z, bias_tab, sinks)
    kst, fb, eb, l_same, l_swap = ssm_prep(a_re, a_im, log_step, b_re, b_im, c_re, c_im, scan_steps)
    y = ssm_chunked(z, kst, fb, eb, l_same, l_swap, d_skip, scan_steps)
    return [y_attn, gelu_glu(y, w_glu, b_glu)]


def kernel(x, rel_bias, norm_mix_pre, norm_mix_post, norm_ffn_pre, norm_ffn_post, e_w_in, e_pool_w, e_pool_scale, e_conv_w, e_w_out, o_w_in, o_sinks, o_a_re, o_a_im, o_log_step, o_b_re, o_b_im, o_c_re, o_c_im, o_d, o_glu_w, o_glu_b, o_w_out, f_w_up, f_conv_w, f_w_down):
    bf = lambda a: a.astype(BF16)
    xr = x.reshape(BATCH * SEQ, D_MODEL).astype(F32)
    bias_tab = rel_bias_table(rel_bias.astype(F32), _t5_bucket_table())
    h = rms_norm_bf16(xr, norm_mix_pre[0].astype(F32))
    for i in range(DEPTH):
        j = i // 2
        if i % 2 == 0:
            y = _even_mixer(h, bf(e_w_in[j]), bf(e_pool_w[j]), e_pool_scale[j].astype(F32),
                            e_conv_w[j].astype(F32))
            w_out = bf(e_w_out[j])
        else:
            y = _odd_mixer(h, bf(o_w_in[j]), o_sinks[j].astype(F32), bias_tab,
                           o_a_re[j], o_a_im[j], o_log_step[j], o_b_re[j], o_b_im[j],
                           o_c_re[j], o_c_im[j], o_d[j], bf(o_glu_w[j]), o_glu_b[j].astype(F32))
            w_out = bf(o_w_out[j])
        xr, h = outproj_residual_norm(y, w_out, xr, norm_mix_post[i].astype(F32),
                                      norm_ffn_pre[i].astype(F32))
        act = ffn_up(h, bf(f_w_up[i]), f_conv_w[i].astype(F32))
        g_next = norm_mix_pre[i + 1].astype(F32) if i + 1 < DEPTH else None
        xr, h = outproj_residual_norm([act], bf(f_w_down[i]), xr, norm_ffn_post[i].astype(F32), g_next)
    return xr.reshape(x.shape)
```

```python
import functools
import math

import jax
import jax.numpy as jnp
from jax import lax
from jax.experimental import pallas as pl
from jax.experimental.pallas import tpu as pltpu

F32 = jnp.float32
BF16 = jnp.bfloat16

D_MODEL = 4096
BATCH = 4
SEQ = 4096
DEPTH = 4

POOL_WIDTH = D_MODEL // 2
POOL_WINDOWS = (2, 4, 8, 16)
POOL_GROUP = POOL_WIDTH // len(POOL_WINDOWS)
SCONV_WIDTH = D_MODEL // 2
HEAD_DIM = 64
N_Q_HEADS = D_MODEL // 2 // HEAD_DIM
N_KV_HEADS = N_Q_HEADS // 4
Q_PER_KV = N_Q_HEADS // N_KV_HEADS
WINDOW = 128
BLOCK = 128
Q_WIDTH = N_Q_HEADS * HEAD_DIM
KV_WIDTH = N_KV_HEADS * HEAD_DIM
SSM_WIDTH = D_MODEL // 4
SSM_GROUP_CH = 16
SSM_GROUPS = SSM_WIDTH // SSM_GROUP_CH
SSM_STATE = 64
ODD_OUT = Q_WIDTH + SSM_WIDTH
REL_BUCKETS = 32
REL_MAX_DIST = 128
D_FF = ((8 * D_MODEL // 3 + 255) // 256) * 256
EPS = 1e-6

CHUNK = 16
CHUNK_W = CHUNK * SSM_GROUP_CH

HALO = 16
CONV_HALO = 8
NEG = -1e30

TM_NORM = 256
TM_MIX = 512
TM_OUT = 512
TK_OUT = 1024
TM_FFN = 1024
FFN_TILES = 2
FFN_ROW_SPLIT = (1, 2)
TM_MM = 1024
TM_GLU = 512
TN_FFN = 256
ATTN_LOOKAHEAD = 5
SSM_LANE_GROUPS = 8

VMEM_LIMIT = 56 * 1024 * 1024
VMEM_LIMIT_OUTPROJ = 62 * 1024 * 1024


def _cparams(sem, vmem=VMEM_LIMIT):
    return pltpu.CompilerParams(dimension_semantics=sem, vmem_limit_bytes=vmem)


def _dot(a, b):
    return jnp.dot(a, b, preferred_element_type=F32)


def _load_history(ext_view, carry_view, is_first, rows):
    @pl.when(is_first)
    def _():
        ext_view[0:rows, :] = jnp.zeros((rows, ext_view.shape[1]), F32)

    @pl.when(jnp.logical_not(is_first))
    def _():
        ext_view[0:rows, :] = carry_view[...]


def _rms_kernel(x_ref, g_ref, o_ref):
    x = x_ref[...]
    r = lax.rsqrt(jnp.mean(x * x, axis=-1, keepdims=True) + EPS)
    o_ref[...] = (x * r * g_ref[...]).astype(o_ref.dtype)


def rms_norm_bf16(x, g):
    m, d = x.shape
    tm = min(TM_NORM, m)
    return pl.pallas_call(
        _rms_kernel,
        grid=(m // tm,),
        in_specs=[pl.BlockSpec((tm, d), lambda i: (i, 0)),
                  pl.BlockSpec((1, d), lambda i: (0, 0))],
        out_specs=pl.BlockSpec((tm, d), lambda i: (i, 0)),
        out_shape=jax.ShapeDtypeStruct((m, d), BF16),
        compiler_params=_cparams(("arbitrary",)),
        name="rms_norm",
    )(x, g.reshape(1, d))


def _pool_kernel(h_ref, w_ref, pw_ref, sc_ref, o_ref, carry_ref, ext_ref, *, tm, tiles_per_seq):
    i = pl.program_id(0)
    j = pl.program_id(1)
    seq_tile = i % tiles_per_seq
    _load_history(ext_ref, carry_ref.at[j], seq_tile == 0, HALO)
    a = _dot(h_ref[...], w_ref[...])
    ext_ref[HALO:, :] = a
    carry_ref[j] = a[tm - HALO:, :]
    t = seq_tile * tm + lax.broadcasted_iota(jnp.int32, (tm, 1), 0)
    for gi, w in enumerate(POOL_WINDOWS):
        @pl.when(j == gi)
        def _():
            cur = ext_ref[HALO:, :]
            s = cur
            for d in range(1, w):
                s = s + ext_ref[HALO - d:HALO - d + tm, :]
            den = jnp.minimum(t + 1, w).astype(F32)
            p = s / den - cur
            y = _dot(p.astype(BF16), pw_ref[...]) * sc_ref[...]
            o_ref[...] = y.astype(o_ref.dtype)


def pool_mixer(h, w_in, pool_w, pool_scale):
    m, d = h.shape
    tn = POOL_GROUP
    tm = min(TM_MIX, SEQ)
    ng = len(POOL_WINDOWS)
    return pl.pallas_call(
        functools.partial(_pool_kernel, tm=tm, tiles_per_seq=SEQ // tm),
        grid=(m // tm, ng),
        in_specs=[pl.BlockSpec((tm, d), lambda i, j: (i, 0)),
                  pl.BlockSpec((d, tn), lambda i, j: (0, j)),
                  pl.BlockSpec((None, tn, tn), lambda i, j: (j, 0, 0)),
                  pl.BlockSpec((1, tn), lambda i, j: (0, j))],
        out_specs=pl.BlockSpec((tm, tn), lambda i, j: (i, j)),
        out_shape=jax.ShapeDtypeStruct((m, POOL_WIDTH), BF16),
        scratch_shapes=[pltpu.VMEM((ng, HALO, tn), F32),
                        pltpu.VMEM((tm + HALO, tn), F32)],
        compiler_params=_cparams(("arbitrary", "arbitrary")),
        name="pool_mixer",
    )(h, w_in, pool_w, pool_scale.reshape(1, POOL_WIDTH))


def _sconv_kernel(h_ref, whb_ref, wgb_ref, wgc_ref, cw_ref, o_ref, carry_ref, ext_ref,
                  *, tm, tiles_per_seq):
    i = pl.program_id(0)
    j = pl.program_id(1)
    _load_history(ext_ref, carry_ref.at[j], i % tiles_per_seq == 0, CONV_HALO)
    h = h_ref[...]
    u = _dot(h, wgc_ref[...]) * _dot(h, whb_ref[...])
    ext_ref[CONV_HALO:, :] = u
    carry_ref[j] = u[tm - CONV_HALO:, :]
    cw = cw_ref[...]
    conv = (cw[2:3, :] * ext_ref[CONV_HALO:, :]
            + cw[1:2, :] * ext_ref[CONV_HALO - 1:CONV_HALO - 1 + tm, :]
            + cw[0:1, :] * ext_ref[CONV_HALO - 2:CONV_HALO - 2 + tm, :])
    o_ref[...] = (_dot(h, wgb_ref[...]) * conv).astype(o_ref.dtype)


def sconv_mixer(h, w_in, conv_w):
    m, d = h.shape
    tn = min(512, SCONV_WIDTH)
    tm = min(TM_MIX, SEQ)
    nb = SCONV_WIDTH // tn
    off = POOL_WIDTH // tn
    return pl.pallas_call(
        functools.partial(_sconv_kernel, tm=tm, tiles_per_seq=SEQ // tm),
        grid=(m // tm, nb),
        in_specs=[pl.BlockSpec((tm, d), lambda i, j: (i, 0)),
                  pl.BlockSpec((d, tn), lambda i, j: (0, off + j)),
                  pl.BlockSpec((d, tn), lambda i, j: (0, off + nb + j)),
                  pl.BlockSpec((d, tn), lambda i, j: (0, off + 2 * nb + j)),
                  pl.BlockSpec((3, tn), lambda i, j: (0, j))],
        out_specs=pl.BlockSpec((tm, tn), lambda i, j: (i, j)),
        out_shape=jax.ShapeDtypeStruct((m, SCONV_WIDTH), BF16),
        scratch_shapes=[pltpu.VMEM((nb, CONV_HALO, tn), F32),
                        pltpu.VMEM((tm + CONV_HALO, tn), F32)],
        compiler_params=_cparams(("arbitrary", "arbitrary")),
        name="sconv_mixer",
    )(h, w_in, w_in, w_in, conv_w)


def _outproj_kernel(*refs, nks, tail_rows, emit_h):
    ns = len(nks)
    y_refs = refs[:ns]
    w_ref, x_ref, gpost_ref, gpre_ref, xo_ref = refs[ns:ns + 5]
    ho_ref = refs[ns + 5] if emit_h else None
    nk = sum(nks)
    k = pl.program_id(1)

    @pl.when(k == 0)
    def _():
        xo_ref[...] = jnp.zeros_like(xo_ref)

    def accumulate(y_ref, masked):
        w = w_ref[...]
        if masked:
            row = lax.broadcasted_iota(jnp.int32, (w.shape[0], 1), 0)
            w = jnp.where(row < tail_rows, w, jnp.zeros_like(w))
        xo_ref[...] += _dot(y_ref[...], w)

    if ns == 1 and tail_rows is None:
        accumulate(y_refs[0], False)
    else:
        k0 = 0
        for s, nk_s in enumerate(nks):
            has_tail = tail_rows is not None and s == ns - 1
            hi = k0 + nk_s - (1 if has_tail else 0)
            if hi > k0:
                pl.when((k >= k0) & (k < hi))(functools.partial(accumulate, y_refs[s], False))
            if has_tail:
                pl.when(k == nk - 1)(functools.partial(accumulate, y_refs[s], True))
            k0 += nk_s

    @pl.when(k == nk - 1)
    def _():
        mix = xo_ref[...]
        r = lax.rsqrt(jnp.mean(mix * mix, axis=-1, keepdims=True) + EPS)
        xn = x_ref[...] + mix * r * gpost_ref[...]
        xo_ref[...] = xn
        if emit_h:
            r2 = lax.rsqrt(jnp.mean(xn * xn, axis=-1, keepdims=True) + EPS)
            ho_ref[...] = (xn * r2 * gpre_ref[...]).astype(ho_ref.dtype)


def outproj_residual_norm(ys, w, layer, x, g_post, g_pre_next):
    m = x.shape[0]
    _, krows, d = w.shape
    tm = min(TM_OUT, m)
    tk = next(t for t in (TK_OUT, 512, 256) if all(y.shape[1] % t == 0 for y in ys))
    nks = tuple(y.shape[1] // tk for y in ys)
    nk = sum(nks)
    tail = krows - (nk - 1) * tk
    assert 0 < tail <= tk
    emit_h = g_pre_next is not None
    if not emit_h:
        g_pre_next = g_post
    out_shape = [jax.ShapeDtypeStruct((m, d), F32)]
    out_specs = [pl.BlockSpec((tm, d), lambda i, k: (i, 0))]
    if emit_h:
        out_shape.append(jax.ShapeDtypeStruct((m, d), BF16))
        out_specs.append(pl.BlockSpec((tm, d), lambda i, k: (i, 0)))
    y_specs = []
    k0 = 0
    for nk_s in nks:
        y_specs.append(pl.BlockSpec(
            (tm, tk), functools.partial(lambda i, k, k0, n: (i, jnp.clip(k - k0, 0, n - 1)), k0=k0, n=nk_s)))
        k0 += nk_s
    res = pl.pallas_call(
        functools.partial(_outproj_kernel, nks=nks, tail_rows=None if tail == tk else tail, emit_h=emit_h),
        grid=(m // tm, nk),
        in_specs=y_specs + [
            pl.BlockSpec((None, tk, d), lambda i, k: (layer, k, 0)),
            pl.BlockSpec((tm, d), lambda i, k: (i, 0), pipeline_mode=pl.Buffered(1)),
            pl.BlockSpec((1, d), lambda i, k: (0, 0)),
            pl.BlockSpec((1, d), lambda i, k: (0, 0))],
        out_specs=out_specs,
        out_shape=out_shape,
        compiler_params=_cparams(("arbitrary", "arbitrary"), VMEM_LIMIT_OUTPROJ),
        name="outproj_norm",
    )(*ys, w, x, g_post.reshape(1, d), g_pre_next.reshape(1, d))
    return (res[0], res[1]) if emit_h else (res[0], None)


def _ffn_pad():
    step = FFN_TILES * TN_FFN
    return -(-D_FF // step) * step


def _ffn_up_kernel(h_ref, *refs, tm, tn, tiles, n_valid, tiles_per_seq):
    w_refs = refs[:2 * tiles]
    c_refs = refs[2 * tiles:4 * tiles]
    o_ref, carry_ref, ext_ref = refs[4 * tiles:]
    i = pl.program_id(0)
    j = pl.program_id(1)
    _load_history(ext_ref, carry_ref.at[j], i % tiles_per_seq == 0, CONV_HALO)
    bounds = [0] + [tm * f // FFN_ROW_SPLIT[-1] for f in FFN_ROW_SPLIT]
    for r0, r1 in zip(bounds[:-1], bounds[1:]):
        rows = r1 - r0
        lo = CONV_HALO + r0
        hc = h_ref[r0:r1, :]
        conv = []
        for idx in range(2 * tiles):
            col = idx * tn
            z = _dot(hc, w_refs[idx][...])
            ext_ref[lo:lo + rows, col:col + tn] = z
            cw = c_refs[idx][...]
            conv.append(cw[2:3, :] * z
                        + cw[1:2, :] * ext_ref[lo - 1:lo - 1 + rows, col:col + tn]
                        + cw[0:1, :] * ext_ref[lo - 2:lo - 2 + rows, col:col + tn])
        for t in range(tiles):
            gate, up = conv[t], conv[tiles + t]
            act = gate * (1.0 / (1.0 + jnp.exp(-gate))) * up
            act = jnp.where(j * tiles + t < n_valid, act, 0.0)
            o_ref[r0:r1, t * tn:(t + 1) * tn] = act.astype(o_ref.dtype)
    carry_ref[j] = ext_ref[tm:tm + CONV_HALO, :]


def ffn_up(h, w_up, conv_w, layer):
    m, d = h.shape
    tn = TN_FFN
    tiles = FFN_TILES
    tm = min(TM_FFN, SEQ)
    n_valid = D_FF // tn
    nsteps = _ffn_pad() // (tiles * tn)

    def col(off, t):
        return lambda i, j: (layer, 0, off + jnp.minimum(j * tiles + t, n_valid - 1))

    offsets = [(0, t) for t in range(tiles)] + [(n_valid, t) for t in range(tiles)]
    return pl.pallas_call(
        functools.partial(_ffn_up_kernel, tm=tm, tn=tn, tiles=tiles, n_valid=n_valid,
                          tiles_per_seq=SEQ // tm),
        grid=(m // tm, nsteps),
        in_specs=([pl.BlockSpec((tm, d), lambda i, j: (i, 0))]
                  + [pl.BlockSpec((None, d, tn), col(off, t)) for off, t in offsets]
                  + [pl.BlockSpec((None, 3, tn), col(off, t)) for off, t in offsets]),
        out_specs=pl.BlockSpec((tm, tiles * tn), lambda i, j: (i, j)),
        out_shape=jax.ShapeDtypeStruct((m, nsteps * tiles * tn), BF16),
        scratch_shapes=[pltpu.VMEM((nsteps, CONV_HALO, 2 * tiles * tn), F32),
                        pltpu.VMEM((tm + CONV_HALO, 2 * tiles * tn), F32)],
        compiler_params=_cparams(("arbitrary", "arbitrary")),
        name="ffn_up",
    )(h, *([w_up] * (2 * tiles)), *([conv_w] * (2 * tiles)))


def _mm_kernel(a_ref, w_ref, o_ref):
    o_ref[...] = _dot(a_ref[...], w_ref[...]).astype(o_ref.dtype)


def matmul_bf16(a, w):
    m, kdim = a.shape
    n = w.shape[1]
    tm = min(TM_MM, m)
    tn = min(512, n)
    return pl.pallas_call(
        _mm_kernel,
        grid=(m // tm, n // tn),
        in_specs=[pl.BlockSpec((tm, kdim), lambda i, j: (i, 0)),
                  pl.BlockSpec((kdim, tn), lambda i, j: (0, j))],
        out_specs=pl.BlockSpec((tm, tn), lambda i, j: (i, j)),
        out_shape=jax.ShapeDtypeStruct((m, n), BF16),
        compiler_params=_cparams(("arbitrary", "arbitrary")),
        name="matmul",
    )(a, w)


def _bias_kernel(rb_ref, bucket_ref, o_ref):
    variant = pl.program_id(0)
    h = pl.program_id(1)
    bucket = bucket_ref[...]
    acc = jnp.zeros(bucket.shape, F32)
    for b in range(REL_BUCKETS):
        acc = jnp.where(bucket == b, rb_ref[b, h], acc)
    qi = lax.broadcasted_iota(jnp.int32, bucket.shape, 0)
    kj = lax.broadcasted_iota(jnp.int32, bucket.shape, 1)
    dist = qi + BLOCK - kj
    valid = (dist >= 0) & (dist < WINDOW) & ((kj >= BLOCK) | (variant > 0))
    o_ref[...] = jnp.where(valid, acc, NEG)


def rel_bias_table(rel_bias, bucket):
    return pl.pallas_call(
        _bias_kernel,
        grid=(2, N_Q_HEADS),
        in_specs=[pl.BlockSpec(memory_space=pltpu.SMEM),
                  pl.BlockSpec((BLOCK, 2 * BLOCK), lambda v, h: (0, 0))],
        out_specs=pl.BlockSpec((None, None, BLOCK, 2 * BLOCK), lambda v, h: (v, h, 0, 0)),
        out_shape=jax.ShapeDtypeStruct((2, N_Q_HEADS, BLOCK, 2 * BLOCK), F32),
        compiler_params=_cparams(("arbitrary", "arbitrary")),
        name="rel_bias_table",
    )(rel_bias, bucket)


def _attn_kernel(sink_ref, q_ref, kp_ref, kc_ref, vp_ref, vc_ref, bias_ref, o_ref):
    k = jnp.concatenate([kp_ref[...], kc_ref[...]], axis=0)
    v = jnp.concatenate([vp_ref[...], vc_ref[...]], axis=0)
    scale = HEAD_DIM ** -0.5
    pair = 2 * HEAD_DIM
    low = lax.broadcasted_iota(jnp.int32, (1, pair), 1) < HEAD_DIM
    zero = jnp.zeros((), BF16)
    ones = jnp.ones((2 * BLOCK, pair), BF16)

    def swap_halves(x):
        return pltpu.bitcast(pltpu.roll(pltpu.bitcast(x, jnp.int32), HEAD_DIM, 1), BF16)

    kv = {}
    for gp in range(N_KV_HEADS // 2):
        kblk = k[:, gp * pair:(gp + 1) * pair]
        vblk = v[:, gp * pair:(gp + 1) * pair]
        kswp = swap_halves(kblk)
        vswp = swap_halves(vblk)
        for e in range(2):
            k_lo, k_hi = (kblk, kswp) if e == 0 else (kswp, kblk)
            v_lo = jnp.where(low, vblk if e == 0 else vswp, zero)
            v_hi = jnp.where(low, zero, vswp if e == 0 else vblk)
            kv[2 * gp + e] = ((k_lo, v_lo), (k_hi, v_hi))

    def scores(h):
        hp, half = divmod(h, 2)
        qp = q_ref[:, hp * pair:(hp + 1) * pair] * scale
        qm = jnp.where(low, qp, zero) if half == 0 else jnp.where(low, zero, qp)
        kk = kv[h // Q_PER_KV][half][0]
        return lax.dot_general(qm, kk, (((1,), (1,)), ((), ())), preferred_element_type=F32) + bias_ref[h]

    pending = [scores(h) for h in range(ATTN_LOOKAHEAD)]
    acc = None
    for h in range(N_Q_HEADS):
        s = pending.pop(0)
        if h + ATTN_LOOKAHEAD < N_Q_HEADS:
            pending.append(scores(h + ATTN_LOOKAHEAD))
        hp, half = divmod(h, 2)
        vv = kv[h // Q_PER_KV][half][1]
        sk = sink_ref[h]
        mx = jnp.maximum(jnp.max(s, axis=-1, keepdims=True), sk)
        p = jnp.exp(s - mx).astype(BF16)
        den = _dot(p, ones) + jnp.exp(sk - mx)
        o = _dot(p, vv) / den
        if half == 0:
            acc = o
        else:
            o_ref[:, hp * pair:(hp + 1) * pair] = (acc + o).astype(o_ref.dtype)


def swa_attention(z, bias_tab, sinks):
    m = z.shape[0]
    nb = SEQ // BLOCK
    kcol = Q_WIDTH // KV_WIDTH
    return pl.pallas_call(
        _attn_kernel,
        grid=(BATCH, nb),
        in_specs=[pl.BlockSpec(memory_space=pltpu.SMEM),
                  pl.BlockSpec((BLOCK, Q_WIDTH), lambda b, n: (b * nb + n, 0)),
                  pl.BlockSpec((BLOCK, KV_WIDTH), lambda b, n: (b * nb + jnp.maximum(n - 1, 0), kcol)),
                  pl.BlockSpec((BLOCK, KV_WIDTH), lambda b, n: (b * nb + n, kcol)),
                  pl.BlockSpec((BLOCK, KV_WIDTH), lambda b, n: (b * nb + jnp.maximum(n - 1, 0), kcol + 1)),
                  pl.BlockSpec((BLOCK, KV_WIDTH), lambda b, n: (b * nb + n, kcol + 1)),
                  pl.BlockSpec((None, N_Q_HEADS, BLOCK, 2 * BLOCK),
                               lambda b, n: (jnp.minimum(n, 1), 0, 0, 0))],
        out_specs=pl.BlockSpec((BLOCK, Q_WIDTH), lambda b, n: (b * nb + n, 0)),
        out_shape=jax.ShapeDtypeStruct((m, Q_WIDTH), BF16),
        compiler_params=_cparams(("arbitrary", "arbitrary")),
        name="swa_attention",
    )(sinks, z, z, z, z, z, bias_tab)


def _ssm_prep_kernel(are_l_ref, aim_l_ref, ls_l_ref, are_s_ref, aim_s_ref, ls_s_ref,
                     btr_ref, bti_ref, ctr_ref, cti_ref,
                     kt_ref, ftr_ref, fti_ref, etr_ref, eti_ref, lr_ref, li_ref, *, scan_steps):
    are = are_l_ref[...]
    aim = aim_l_ref[...]
    dt = jnp.exp(ls_l_ref[...])
    lre = are * dt
    th = aim * dt
    mag = jnp.exp(lre)
    lam_r = mag * jnp.cos(th)
    lam_i = mag * jnp.sin(th)
    den = are * are + aim * aim
    n_r = lam_r - 1.0
    n_i = lam_i
    f_r = (n_r * are + n_i * aim) / den
    f_i = (n_i * are - n_r * aim) / den
    btr = btr_ref[...]
    bti = bti_ref[...]
    bb_r = f_r * btr - f_i * bti
    bb_i = f_r * bti + f_i * btr
    kk = (CHUNK - 1 - lax.broadcasted_iota(jnp.int32, (CHUNK, SSM_STATE), 0)).astype(F32)
    pmag = jnp.exp(kk * lre)
    pw_r = pmag * jnp.cos(kk * th)
    pw_i = pmag * jnp.sin(kk * th)
    for s in range(CHUNK):
        pr = pw_r[s:s + 1, :]
        pi = pw_i[s:s + 1, :]
        ftr_ref[s * SSM_GROUP_CH:(s + 1) * SSM_GROUP_CH, :] = bb_r * pr - bb_i * pi
        fti_ref[s * SSM_GROUP_CH:(s + 1) * SSM_GROUP_CH, :] = bb_r * pi + bb_i * pr
    cmag = jnp.exp(float(CHUNK) * lre)
    cr = cmag * jnp.cos(float(CHUNK) * th)
    ci = cmag * jnp.sin(float(CHUNK) * th)
    for k in range(scan_steps):
        lr_ref[k:k + 1, :] = cr
        li_ref[k:k + 1, :] = ci
        cr, ci = cr * cr - ci * ci, 2.0 * cr * ci
    are_s = are_s_ref[...]
    aim_s = aim_s_ref[...]
    dt_s = jnp.exp(ls_s_ref[...])
    lre_s = are_s * dt_s
    th_s = aim_s * dt_s
    kl = (lax.broadcasted_iota(jnp.int32, (1, CHUNK_W), 1) // SSM_GROUP_CH).astype(F32)
    ctr = ctr_ref[...]
    cti = cti_ref[...]

    def c_times_power(k0):
        e = kl + k0
        mg = jnp.exp(e * lre_s)
        pr = mg * jnp.cos(e * th_s)
        pi = mg * jnp.sin(e * th_s)
        return ctr * pr - cti * pi, ctr * pi + cti * pr

    w0_r, w0_i = c_times_power(0.0)
    hi = lax.Precision.HIGHEST
    kt_ref[...] = (jnp.dot(bb_r, w0_r, precision=hi, preferred_element_type=F32)
                   - jnp.dot(bb_i, w0_i, precision=hi, preferred_element_type=F32))
    w1_r, w1_i = c_times_power(1.0)
    etr_ref[...] = w1_r
    eti_ref[...] = -w1_i


def ssm_prep(a_re, a_im, log_step, b_re, b_im, c_re, c_im, scan_steps):
    g, p, h = SSM_GROUPS, SSM_STATE, SSM_GROUP_CH
    ls = log_step.astype(F32)
    lane = lambda a: a.astype(F32).reshape(g, 1, p)
    sub = lambda a: a.astype(F32).reshape(g, p, 1)
    ls_l = jnp.broadcast_to(ls[:, None, None], (g, 1, p))
    ls_s = jnp.broadcast_to(ls[:, None, None], (g, p, 1))
    bt = lambda b: jnp.swapaxes(b.astype(F32), 1, 2)
    ct = lambda c: jnp.tile(jnp.swapaxes(c.astype(F32), 1, 2), (1, 1, CHUNK))
    spec3 = lambda s1, s2: pl.BlockSpec((None, s1, s2), lambda i: (i, 0, 0))
    outs = pl.pallas_call(
        functools.partial(_ssm_prep_kernel, scan_steps=scan_steps),
        grid=(g,),
        in_specs=[spec3(1, p), spec3(1, p), spec3(1, p), spec3(p, 1), spec3(p, 1), spec3(p, 1),
                  spec3(h, p), spec3(h, p), spec3(p, CHUNK_W), spec3(p, CHUNK_W)],
        out_specs=[spec3(h, CHUNK_W), spec3(CHUNK_W, p), spec3(CHUNK_W, p),
                   spec3(p, CHUNK_W), spec3(p, CHUNK_W), spec3(scan_steps, p), spec3(scan_steps, p)],
        out_shape=[jax.ShapeDtypeStruct((g, h, CHUNK_W), F32),
                   jax.ShapeDtypeStruct((g, CHUNK_W, p), F32),
                   jax.ShapeDtypeStruct((g, CHUNK_W, p), F32),
                   jax.ShapeDtypeStruct((g, p, CHUNK_W), F32),
                   jax.ShapeDtypeStruct((g, p, CHUNK_W), F32),
                   jax.ShapeDtypeStruct((g, scan_steps, p), F32),
                   jax.ShapeDtypeStruct((g, scan_steps, p), F32)],
        compiler_params=_cparams(("arbitrary",)),
        name="ssm_prep",
    )(lane(a_re), lane(a_im), ls_l, sub(a_re), sub(a_im), ls_s,
      bt(b_re), bt(b_im), ct(c_re), ct(c_im))
    kt, ftr, fti, etr, eti, lr, li = outs
    gl = SSM_LANE_GROUPS
    nb = g // gl
    p2 = 2 * p
    same = jnp.eye(gl, dtype=bool)
    kt5 = kt.reshape(nb, gl, h, CHUNK, h)[:, :, :, ::-1, :]
    kst = jnp.where(same[None, None, :, None, :, None],
                    kt5.transpose(0, 3, 1, 2, 4)[:, :, :, :, None, :], 0.0)
    kst = kst.reshape(nb, CHUNK * gl * h, gl * h)
    ft5 = jnp.concatenate([ftr, fti], axis=2).reshape(nb, gl, CHUNK, h, p2)
    fb = jnp.where(same[None, None, :, None, :, None],
                   ft5.transpose(0, 2, 1, 3, 4)[:, :, :, :, None, :], 0.0)
    fb = fb.reshape(nb, CHUNK * gl * h, gl * p2)
    et5 = jnp.concatenate([etr, eti], axis=1).reshape(nb, gl, p2, CHUNK, h)
    eb = jnp.where(same[None, :, None, None, :, None], et5[:, :, :, :, None, :], 0.0)
    eb = eb.reshape(nb, gl * p2, CHUNK * gl * h)
    lanes = lambda a: a.reshape(nb, gl, scan_steps, p2).transpose(0, 2, 1, 3).reshape(nb, scan_steps, gl * p2)
    l_same = lanes(jnp.concatenate([lr, lr], axis=2))
    l_swap = lanes(jnp.concatenate([-li, li], axis=2))
    return kst.astype(BF16), fb.astype(BF16), eb.astype(BF16), l_same, l_swap


def _ssm_kernel(z_ref, kst_ref, fb_ref, eb_ref, ls_ref, lw_ref, d_ref, y_ref, u32_ref, s_ref,
                *, scan_steps):
    rows = z_ref.shape[0] // CHUNK
    lanes = z_ref.shape[1]
    p2 = 2 * SSM_STATE
    u32_ref[...] = z_ref[...].astype(F32)

    def tokens(t):
        return u32_ref[pl.ds(t, rows, stride=CHUNK), :]

    u_all = jnp.concatenate([tokens(t).astype(BF16) for t in range(CHUNK)], axis=1)
    s_ref[...] = _dot(u_all, fb_ref[...])
    intra = [_dot(u_all[:, :(t + 1) * lanes], kst_ref[(CHUNK - 1 - t) * lanes:, :]) for t in range(CHUNK)]
    ridx = lax.broadcasted_iota(jnp.int32, (rows, 1), 0)
    prevs = []
    for g in range(SSM_LANE_GROUPS):
        cols = slice(g * p2, (g + 1) * p2)
        x = s_ref[:, cols]
        for k in range(scan_steps):
            d = 1 << k
            sh = jnp.where(ridx >= d, pltpu.roll(x, d, 0), 0.0)
            x = x + ls_ref[k:k + 1, cols] * sh + lw_ref[k:k + 1, cols] * pltpu.roll(sh, SSM_STATE, 1)
        prevs.append(jnp.where(ridx >= 1, pltpu.roll(x, 1, 0), 0.0).astype(BF16))
    y_state = _dot(jnp.concatenate(prevs, axis=1), eb_ref[...])
    for t in range(CHUNK):
        y = y_state[:, t * lanes:(t + 1) * lanes] + intra[t] + d_ref[...] * tokens(t)
        y_ref[pl.ds(t, rows, stride=CHUNK), :] = y


def ssm_chunked(z, kst, fb, eb, l_same, l_swap, d_skip, scan_steps):
    m = z.shape[0]
    nb = kst.shape[0]
    lanes = SSM_LANE_GROUPS * SSM_GROUP_CH
    ucol = (Q_WIDTH + 2 * KV_WIDTH) // lanes
    wspec = lambda a: pl.BlockSpec((None,) + a.shape[1:], lambda i, b: (i, 0, 0))
    return pl.pallas_call(
        functools.partial(_ssm_kernel, scan_steps=scan_steps),
        grid=(nb, m // SEQ),
        in_specs=[pl.BlockSpec((SEQ, lanes), lambda i, b: (b, ucol + i)),
                  wspec(kst), wspec(fb), wspec(eb), wspec(l_same), wspec(l_swap),
                  pl.BlockSpec((1, lanes), lambda i, b: (0, i))],
        out_specs=pl.BlockSpec((SEQ, lanes), lambda i, b: (b, i)),
        out_shape=jax.ShapeDtypeStruct((m, SSM_WIDTH), F32),
        scratch_shapes=[pltpu.VMEM((SEQ, lanes), F32),
                        pltpu.VMEM((SEQ // CHUNK, SSM_LANE_GROUPS * 2 * SSM_STATE), F32)],
        compiler_params=_cparams(("arbitrary", "arbitrary")),
        name="ssm_chunked",
    )(z, kst, fb, eb, l_same, l_swap, d_skip.astype(F32).reshape(1, SSM_WIDTH))


def _glu_kernel(y_ref, w_ref, b_ref, o_ref):
    y = y_ref[...]
    gel = 0.5 * y * (1.0 + jnp.tanh(math.sqrt(2.0 / math.pi) * (y + 0.044715 * (y * y * y))))
    t = _dot(gel.astype(BF16), w_ref[...]) + b_ref[...]
    o_ref[...] = (gel * (1.0 / (1.0 + jnp.exp(-t)))).astype(o_ref.dtype)


def gelu_glu(y, w_glu, b_glu):
    m = y.shape[0]
    tm = min(TM_GLU, m)
    return pl.pallas_call(
        _glu_kernel,
        grid=(m // tm,),
        in_specs=[pl.BlockSpec((tm, SSM_WIDTH), lambda i: (i, 0)),
                  pl.BlockSpec((SSM_WIDTH, SSM_WIDTH), lambda i: (0, 0)),
                  pl.BlockSpec((1, SSM_WIDTH), lambda i: (0, 0))],
        out_specs=pl.BlockSpec((tm, SSM_WIDTH), lambda i: (i, 0)),
        out_shape=jax.ShapeDtypeStruct((m, SSM_WIDTH), BF16),
        compiler_params=_cparams(("arbitrary",)),
        name="gelu_glu",
    )(y, w_glu, b_glu.reshape(1, SSM_WIDTH))


def _t5_bucket_table():
    qi = jnp.arange(BLOCK)[:, None]
    kj = jnp.arange(2 * BLOCK)[None, :]
    n = jnp.maximum(qi + BLOCK - kj, 0)
    max_exact = REL_BUCKETS // 2
    nf = jnp.maximum(n, 1).astype(F32)
    large = max_exact + (jnp.log(nf / max_exact) / math.log(REL_MAX_DIST / max_exact)
                         * (REL_BUCKETS - max_exact)).astype(jnp.int32)
    large = jnp.minimum(large, REL_BUCKETS - 1)
    return jnp.where(n < max_exact, n, large).astype(jnp.int32)


def _even_mixer(h, w_in, pool_w, pool_scale, conv_w):
    return [pool_mixer(h, w_in, pool_w, pool_scale), sconv_mixer(h, w_in, conv_w)]


def _odd_mixer(h, w_in, sinks, bias_tab, a_re, a_im, log_step, b_re, b_im, c_re, c_im,
               d_skip, w_glu, b_glu):
    scan_steps = max(1, (SEQ // CHUNK - 1).bit_length())
    z = matmul_bf16(h, w_in)
    y_attn = swa_attention(z, bias_tab, sinks)
    kst, fb, eb, l_same, l_swap = ssm_prep(a_re, a_im, log_step, b_re, b_im, c_re, c_im, scan_steps)
    y = ssm_chunked(z, kst, fb, eb, l_same, l_swap, d_skip, scan_steps)
    return [y_attn, gelu_glu(y, w_glu, b_glu)]


def kernel(x, rel_bias, norm_mix_pre, norm_mix_post, norm_ffn_pre, norm_ffn_post, e_w_in, e_pool_w, e_pool_scale, e_conv_w, e_w_out, o_w_in, o_sinks, o_a_re, o_a_im, o_log_step, o_b_re, o_b_im, o_c_re, o_c_im, o_d, o_glu_w, o_glu_b, o_w_out, f_w_up, f_conv_w, f_w_down):
    bf = lambda a: a.astype(BF16)
    xr = x.reshape(BATCH * SEQ, D_MODEL).astype(F32)
    bias_tab = rel_bias_table(rel_bias.astype(F32), _t5_bucket_table())
    h = rms_norm_bf16(xr, norm_mix_pre[0].astype(F32))
    w_up_all, w_down_all, fconv_all = bf(f_w_up), bf(f_w_down), f_conv_w.astype(F32)
    e_out_all, o_out_all = bf(e_w_out), bf(o_w_out)
    for i in range(DEPTH):
        j = i // 2
        if i % 2 == 0:
            y = _even_mixer(h, bf(e_w_in[j]), bf(e_pool_w[j]), e_pool_scale[j].astype(F32),
                            e_conv_w[j].astype(F32))
            w_out = e_out_all
        else:
            y = _odd_mixer(h, bf(o_w_in[j]), o_sinks[j].astype(F32), bias_tab,
                           o_a_re[j], o_a_im[j], o_log_step[j], o_b_re[j], o_b_im[j],
                           o_c_re[j], o_c_im[j], o_d[j], bf(o_glu_w[j]), o_glu_b[j].astype(F32))
            w_out = o_out_all
        xr, h = outproj_residual_norm(y, w_out, j, xr, norm_mix_post[i].astype(F32),
                                      norm_ffn_pre[i].astype(F32))
        act = ffn_up(h, w_up_all, fconv_all, i)
        g_next = norm_mix_pre[i + 1].astype(F32) if i + 1 < DEPTH else None
        xr, h = outproj_residual_norm([act], w_down_all, i, xr, norm_ffn_post[i].astype(F32), g_next)
    return xr.reshape(x.shape)
```

```python
import functools
import math

import jax
import jax.numpy as jnp
from jax import lax
from jax.experimental import pallas as pl
from jax.experimental.pallas import tpu as pltpu

F32 = jnp.float32
BF16 = jnp.bfloat16

D_MODEL = 4096
BATCH = 4
SEQ = 4096
DEPTH = 4

POOL_WIDTH = D_MODEL // 2
POOL_WINDOWS = (2, 4, 8, 16)
POOL_GROUP = POOL_WIDTH // len(POOL_WINDOWS)
SCONV_WIDTH = D_MODEL // 2
HEAD_DIM = 64
N_Q_HEADS = D_MODEL // 2 // HEAD_DIM
N_KV_HEADS = N_Q_HEADS // 4
Q_PER_KV = N_Q_HEADS // N_KV_HEADS
WINDOW = 128
BLOCK = 128
Q_WIDTH = N_Q_HEADS * HEAD_DIM
KV_WIDTH = N_KV_HEADS * HEAD_DIM
SSM_WIDTH = D_MODEL // 4
SSM_GROUP_CH = 16
SSM_GROUPS = SSM_WIDTH // SSM_GROUP_CH
SSM_STATE = 64
ODD_OUT = Q_WIDTH + SSM_WIDTH
REL_BUCKETS = 32
REL_MAX_DIST = 128
D_FF = ((8 * D_MODEL // 3 + 255) // 256) * 256
EPS = 1e-6

CHUNK = 16
CHUNK_W = CHUNK * SSM_GROUP_CH

HALO = 16
CONV_HALO = 8
NEG = -1e30

TM_NORM = 256
TM_MIX = 512
TM_OUT = 512
TK_OUT = 1024
TM_FFN = 1024
FFN_TILES = 2
FFN_ROW_SPLIT = (1, 2)
TM_MM = 1024
TM_GLU = 512
TN_FFN = 256
ATTN_LOOKAHEAD = 5
SSM_LANE_GROUPS = 8

VMEM_LIMIT = 56 * 1024 * 1024
VMEM_LIMIT_OUTPROJ = 62 * 1024 * 1024


def _cparams(sem, vmem=VMEM_LIMIT):
    return pltpu.CompilerParams(dimension_semantics=sem, vmem_limit_bytes=vmem)


def _dot(a, b):
    return jnp.dot(a, b, preferred_element_type=F32)


def _load_history(ext_view, carry_view, is_first, rows):
    @pl.when(is_first)
    def _():
        ext_view[0:rows, :] = jnp.zeros((rows, ext_view.shape[1]), F32)

    @pl.when(jnp.logical_not(is_first))
    def _():
        ext_view[0:rows, :] = carry_view[...]


def _rms_kernel(x_ref, g_ref, o_ref):
    x = x_ref[...]
    r = lax.rsqrt(jnp.mean(x * x, axis=-1, keepdims=True) + EPS)
    o_ref[...] = (x * r * g_ref[...]).astype(o_ref.dtype)


def rms_norm_bf16(x, g):
    m, d = x.shape
    tm = min(TM_NORM, m)
    return pl.pallas_call(
        _rms_kernel,
        grid=(m // tm,),
        in_specs=[pl.BlockSpec((tm, d), lambda i: (i, 0)),
                  pl.BlockSpec((1, d), lambda i: (0, 0))],
        out_specs=pl.BlockSpec((tm, d), lambda i: (i, 0)),
        out_shape=jax.ShapeDtypeStruct((m, d), BF16),
        compiler_params=_cparams(("arbitrary",)),
        name="rms_norm",
    )(x, g.reshape(1, d))


def _pool_kernel(h_ref, w_ref, pw_ref, sc_ref, o_ref, carry_ref, ext_ref, *, tm, tiles_per_seq):
    i = pl.program_id(0)
    j = pl.program_id(1)
    seq_tile = i % tiles_per_seq
    _load_history(ext_ref, carry_ref.at[j], seq_tile == 0, HALO)
    half = tm // 2
    for gi, w in enumerate(POOL_WINDOWS):
        @pl.when(j == gi)
        def _():
            for r0 in (0, half):
                ext_ref[HALO + r0:HALO + r0 + half, :] = _dot(h_ref[r0:r0 + half, :], w_ref[...])
            for r0 in (0, half):
                cur = ext_ref[HALO + r0:HALO + r0 + half, :]
                s = cur
                for d in range(1, w):
                    s = s + ext_ref[HALO + r0 - d:HALO + r0 - d + half, :]
                t = seq_tile * tm + r0 + lax.broadcasted_iota(jnp.int32, (half, 1), 0)
                den = jnp.minimum(t + 1, w).astype(F32)
                p = s / den - cur
                y = _dot(p.astype(BF16), pw_ref[...]) * sc_ref[...]
                o_ref[r0:r0 + half, :] = y.astype(o_ref.dtype)
    carry_ref[j] = ext_ref[tm:tm + HALO, :]


def pool_mixer(h, w_in, pool_w, pool_scale):
    m, d = h.shape
    tn = POOL_GROUP
    tm = min(TM_MIX, SEQ)
    ng = len(POOL_WINDOWS)
    return pl.pallas_call(
        functools.partial(_pool_kernel, tm=tm, tiles_per_seq=SEQ // tm),
        grid=(m // tm, ng),
        in_specs=[pl.BlockSpec((tm, d), lambda i, j: (i, 0)),
                  pl.BlockSpec((d, tn), lambda i, j: (0, j)),
                  pl.BlockSpec((None, tn, tn), lambda i, j: (j, 0, 0)),
                  pl.BlockSpec((1, tn), lambda i, j: (0, j))],
        out_specs=pl.BlockSpec((tm, tn), lambda i, j: (i, j)),
        out_shape=jax.ShapeDtypeStruct((m, POOL_WIDTH), BF16),
        scratch_shapes=[pltpu.VMEM((ng, HALO, tn), F32),
                        pltpu.VMEM((tm + HALO, tn), F32)],
        compiler_params=_cparams(("arbitrary", "arbitrary")),
        name="pool_mixer",
    )(h, w_in, pool_w, pool_scale.reshape(1, POOL_WIDTH))


def _sconv_kernel(h_ref, whb_ref, wgb_ref, wgc_ref, cw_ref, o_ref, carry_ref, ext_ref,
                  *, tm, tiles_per_seq):
    i = pl.program_id(0)
    j = pl.program_id(1)
    _load_history(ext_ref, carry_ref.at[j], i % tiles_per_seq == 0, CONV_HALO)
    h = h_ref[...]
    u = _dot(h, wgc_ref[...]) * _dot(h, whb_ref[...])
    ext_ref[CONV_HALO:, :] = u
    carry_ref[j] = u[tm - CONV_HALO:, :]
    cw = cw_ref[...]
    conv = (cw[2:3, :] * ext_ref[CONV_HALO:, :]
            + cw[1:2, :] * ext_ref[CONV_HALO - 1:CONV_HALO - 1 + tm, :]
            + cw[0:1, :] * ext_ref[CONV_HALO - 2:CONV_HALO - 2 + tm, :])
    o_ref[...] = (_dot(h, wgb_ref[...]) * conv).astype(o_ref.dtype)


def sconv_mixer(h, w_in, conv_w):
    m, d = h.shape
    tn = min(512, SCONV_WIDTH)
    tm = min(TM_MIX, SEQ)
    nb = SCONV_WIDTH // tn
    off = POOL_WIDTH // tn
    return pl.pallas_call(
        functools.partial(_sconv_kernel, tm=tm, tiles_per_seq=SEQ // tm),
        grid=(m // tm, nb),
        in_specs=[pl.BlockSpec((tm, d), lambda i, j: (i, 0)),
                  pl.BlockSpec((d, tn), lambda i, j: (0, off + j)),
                  pl.BlockSpec((d, tn), lambda i, j: (0, off + nb + j)),
                  pl.BlockSpec((d, tn), lambda i, j: (0, off + 2 * nb + j)),
                  pl.BlockSpec((3, tn), lambda i, j: (0, j))],
        out_specs=pl.BlockSpec((tm, tn), lambda i, j: (i, j)),
        out_shape=jax.ShapeDtypeStruct((m, SCONV_WIDTH), BF16),
        scratch_shapes=[pltpu.VMEM((nb, CONV_HALO, tn), F32),
                        pltpu.VMEM((tm + CONV_HALO, tn), F32)],
        compiler_params=_cparams(("arbitrary", "arbitrary")),
        name="sconv_mixer",
    )(h, w_in, w_in, w_in, conv_w)


def _outproj_kernel(*refs, nks, tail_rows, emit_h):
    ns = len(nks)
    y_refs = refs[:ns]
    w_ref, x_hbm, gpost_ref, gpre_ref, xo_ref = refs[ns:ns + 5]
    rest = refs[ns + 5:]
    ho_ref = rest[0] if emit_h else None
    x_ref, x_sem = rest[-2:]
    nk = sum(nks)
    assert nk >= 2
    tm = xo_ref.shape[0]
    i = pl.program_id(0)
    k = pl.program_id(1)

    x_copy = pltpu.make_async_copy(x_hbm.at[pl.ds(i * tm, tm), :], x_ref, x_sem)

    def accumulate(y_ref, masked, first):
        w = w_ref[...]
        if masked:
            row = lax.broadcasted_iota(jnp.int32, (w.shape[0], 1), 0)
            w = jnp.where(row < tail_rows, w, jnp.zeros_like(w))
        if first:
            xo_ref[...] = _dot(y_ref[...], w)
        else:
            xo_ref[...] += _dot(y_ref[...], w)

    @pl.when(k == 0)
    def _():
        x_copy.start()
        accumulate(y_refs[0], False, True)

    k0 = 0
    for s, nk_s in enumerate(nks):
        has_tail = tail_rows is not None and s == ns - 1
        lo = max(k0, 1)
        hi = k0 + nk_s - (1 if has_tail else 0)
        if hi > lo:
            pl.when((k >= lo) & (k < hi))(functools.partial(accumulate, y_refs[s], False, False))
        if has_tail:
            pl.when(k == nk - 1)(functools.partial(accumulate, y_refs[s], True, False))
        k0 += nk_s

    @pl.when(k == nk - 1)
    def _():
        x_copy.wait()
        mix = xo_ref[...]
        r = lax.rsqrt(jnp.mean(mix * mix, axis=-1, keepdims=True) + EPS)
        xn = x_ref[...] + mix * r * gpost_ref[...]
        xo_ref[...] = xn
        if emit_h:
            r2 = lax.rsqrt(jnp.mean(xn * xn, axis=-1, keepdims=True) + EPS)
            ho_ref[...] = (xn * r2 * gpre_ref[...]).astype(ho_ref.dtype)


def outproj_residual_norm(ys, w, layer, x, g_post, g_pre_next):
    m = x.shape[0]
    _, krows, d = w.shape
    tm = min(TM_OUT, m)
    tk = next(t for t in (TK_OUT, 512, 256) if all(y.shape[1] % t == 0 for y in ys))
    nks = tuple(y.shape[1] // tk for y in ys)
    nk = sum(nks)
    tail = krows - (nk - 1) * tk
    assert 0 < tail <= tk
    emit_h = g_pre_next is not None
    if not emit_h:
        g_pre_next = g_post
    out_shape = [jax.ShapeDtypeStruct((m, d), F32)]
    out_specs = [pl.BlockSpec((tm, d), lambda i, k: (i, 0))]
    if emit_h:
        out_shape.append(jax.ShapeDtypeStruct((m, d), BF16))
        out_specs.append(pl.BlockSpec((tm, d), lambda i, k: (i, 0)))
    y_specs = []
    k0 = 0
    for nk_s in nks:
        y_specs.append(pl.BlockSpec(
            (tm, tk), functools.partial(lambda i, k, k0, n: (i, jnp.clip(k - k0, 0, n - 1)), k0=k0, n=nk_s)))
        k0 += nk_s
    res = pl.pallas_call(
        functools.partial(_outproj_kernel, nks=nks, tail_rows=None if tail == tk else tail, emit_h=emit_h),
        grid=(m // tm, nk),
        in_specs=y_specs + [
            pl.BlockSpec((None, tk, d), lambda i, k: (layer, k, 0)),
            pl.BlockSpec(memory_space=pl.ANY),
            pl.BlockSpec((1, d), lambda i, k: (0, 0)),
            pl.BlockSpec((1, d), lambda i, k: (0, 0))],
        out_specs=out_specs,
        out_shape=out_shape,
        scratch_shapes=[pltpu.VMEM((tm, d), F32), pltpu.SemaphoreType.DMA(())],
        compiler_params=_cparams(("arbitrary", "arbitrary"), VMEM_LIMIT_OUTPROJ),
        name="outproj_norm",
    )(*ys, w, x, g_post.reshape(1, d), g_pre_next.reshape(1, d))
    return (res[0], res[1]) if emit_h else (res[0], None)


def _ffn_pad():
    step = FFN_TILES * TN_FFN
    return -(-D_FF // step) * step


def _ffn_up_kernel(h_ref, *refs, tm, tn, tiles, n_valid, tiles_per_seq):
    w_refs = refs[:2 * tiles]
    c_refs = refs[2 * tiles:4 * tiles]
    o_ref, carry_ref, ext_ref = refs[4 * tiles:]
    i = pl.program_id(0)
    j = pl.program_id(1)
    _load_history(ext_ref, carry_ref.at[j], i % tiles_per_seq == 0, CONV_HALO)
    bounds = [0] + [tm * f // FFN_ROW_SPLIT[-1] for f in FFN_ROW_SPLIT]
    for r0, r1 in zip(bounds[:-1], bounds[1:]):
        rows = r1 - r0
        lo = CONV_HALO + r0
        hc = h_ref[r0:r1, :]
        conv = []
        for idx in range(2 * tiles):
            col = idx * tn
            z = _dot(hc, w_refs[idx][...])
            ext_ref[lo:lo + rows, col:col + tn] = z
            cw = c_refs[idx][...]
            conv.append(cw[2:3, :] * z
                        + cw[1:2, :] * ext_ref[lo - 1:lo - 1 + rows, col:col + tn]
                        + cw[0:1, :] * ext_ref[lo - 2:lo - 2 + rows, col:col + tn])
        for t in range(tiles):
            gate, up = conv[t], conv[tiles + t]
            act = gate * (1.0 / (1.0 + jnp.exp(-gate))) * up
            act = jnp.where(j * tiles + t < n_valid, act, 0.0)
            o_ref[r0:r1, t * tn:(t + 1) * tn] = act.astype(o_ref.dtype)
    carry_ref[j] = ext_ref[tm:tm + CONV_HALO, :]


def ffn_up(h, w_up, conv_w, layer):
    m, d = h.shape
    tn = TN_FFN
    tiles = FFN_TILES
    tm = min(TM_FFN, SEQ)
    n_valid = D_FF // tn
    nsteps = _ffn_pad() // (tiles * tn)

    def col(off, t):
        return lambda i, j: (layer, 0, off + jnp.minimum(j * tiles + t, n_valid - 1))

    offsets = [(0, t) for t in range(tiles)] + [(n_valid, t) for t in range(tiles)]
    return pl.pallas_call(
        functools.partial(_ffn_up_kernel, tm=tm, tn=tn, tiles=tiles, n_valid=n_valid,
                          tiles_per_seq=SEQ // tm),
        grid=(m // tm, nsteps),
        in_specs=([pl.BlockSpec((tm, d), lambda i, j: (i, 0))]
                  + [pl.BlockSpec((None, d, tn), col(off, t)) for off, t in offsets]
                  + [pl.BlockSpec((None, 3, tn), col(off, t)) for off, t in offsets]),
        out_specs=pl.BlockSpec((tm, tiles * tn), lambda i, j: (i, j)),
        out_shape=jax.ShapeDtypeStruct((m, nsteps * tiles * tn), BF16),
        scratch_shapes=[pltpu.VMEM((nsteps, CONV_HALO, 2 * tiles * tn), F32),
                        pltpu.VMEM((tm + CONV_HALO, 2 * tiles * tn), F32)],
        compiler_params=_cparams(("arbitrary", "arbitrary")),
        name="ffn_up",
    )(h, *([w_up] * (2 * tiles)), *([conv_w] * (2 * tiles)))


def _mm_kernel(a_ref, w_ref, o_ref):
    o_ref[...] = _dot(a_ref[...], w_ref[...]).astype(o_ref.dtype)


def matmul_bf16(a, w):
    m, kdim = a.shape
    n = w.shape[1]
    tm = min(TM_MM, m)
    tn = min(512, n)
    return pl.pallas_call(
        _mm_kernel,
        grid=(m // tm, n // tn),
        in_specs=[pl.BlockSpec((tm, kdim), lambda i, j: (i, 0)),
                  pl.BlockSpec((kdim, tn), lambda i, j: (0, j))],
        out_specs=pl.BlockSpec((tm, tn), lambda i, j: (i, j)),
        out_shape=jax.ShapeDtypeStruct((m, n), BF16),
        compiler_params=_cparams(("arbitrary", "arbitrary")),
        name="matmul",
    )(a, w)


def _bias_kernel(rb_ref, bucket_ref, o_ref):
    variant = pl.program_id(0)
    h = pl.program_id(1)
    bucket = bucket_ref[...]
    acc = jnp.zeros(bucket.shape, F32)
    for b in range(REL_BUCKETS):
        acc = jnp.where(bucket == b, rb_ref[b, h], acc)
    qi = lax.broadcasted_iota(jnp.int32, bucket.shape, 0)
    kj = lax.broadcasted_iota(jnp.int32, bucket.shape, 1)
    dist = qi + BLOCK - kj
    valid = (dist >= 0) & (dist < WINDOW) & ((kj >= BLOCK) | (variant > 0))
    o_ref[...] = jnp.where(valid, acc, NEG)


def rel_bias_table(rel_bias, bucket):
    return pl.pallas_call(
        _bias_kernel,
        grid=(2, N_Q_HEADS),
        in_specs=[pl.BlockSpec(memory_space=pltpu.SMEM),
                  pl.BlockSpec((BLOCK, 2 * BLOCK), lambda v, h: (0, 0))],
        out_specs=pl.BlockSpec((None, None, BLOCK, 2 * BLOCK), lambda v, h: (v, h, 0, 0)),
        out_shape=jax.ShapeDtypeStruct((2, N_Q_HEADS, BLOCK, 2 * BLOCK), F32),
        compiler_params=_cparams(("arbitrary", "arbitrary")),
        name="rel_bias_table",
    )(rel_bias, bucket)


def _attn_kernel(sink_ref, q_ref, kp_ref, kc_ref, vp_ref, vc_ref, bias_ref, o_ref):
    k = jnp.concatenate([kp_ref[...], kc_ref[...]], axis=0)
    v = jnp.concatenate([vp_ref[...], vc_ref[...]], axis=0)
    scale = HEAD_DIM ** -0.5
    pair = 2 * HEAD_DIM
    low = lax.broadcasted_iota(jnp.int32, (1, pair), 1) < HEAD_DIM
    zero = jnp.zeros((), BF16)
    ones = jnp.ones((2 * BLOCK, pair), BF16)

    def swap_halves(x):
        return pltpu.bitcast(pltpu.roll(pltpu.bitcast(x, jnp.int32), HEAD_DIM, 1), BF16)

    kv = {}
    for gp in range(N_KV_HEADS // 2):
        kblk = k[:, gp * pair:(gp + 1) * pair]
        vblk = v[:, gp * pair:(gp + 1) * pair]
        kswp = swap_halves(kblk)
        vswp = swap_halves(vblk)
        for e in range(2):
            k_lo, k_hi = (kblk, kswp) if e == 0 else (kswp, kblk)
            v_lo = jnp.where(low, vblk if e == 0 else vswp, zero)
            v_hi = jnp.where(low, zero, vswp if e == 0 else vblk)
            kv[2 * gp + e] = ((k_lo, v_lo), (k_hi, v_hi))

    def scores(h):
        hp, half = divmod(h, 2)
        qp = q_ref[:, hp * pair:(hp + 1) * pair] * scale
        qm = jnp.where(low, qp, zero) if half == 0 else jnp.where(low, zero, qp)
        kk = kv[h // Q_PER_KV][half][0]
        return lax.dot_general(qm, kk, (((1,), (1,)), ((), ())), preferred_element_type=F32) + bias_ref[h]

    pending = [scores(h) for h in range(ATTN_LOOKAHEAD)]
    acc = None
    for h in range(N_Q_HEADS):
        s = pending.pop(0)
        if h + ATTN_LOOKAHEAD < N_Q_HEADS:
            pending.append(scores(h + ATTN_LOOKAHEAD))
        hp, half = divmod(h, 2)
        vv = kv[h // Q_PER_KV][half][1]
        sk = sink_ref[h]
        mx = jnp.maximum(jnp.max(s, axis=-1, keepdims=True), sk)
        p = jnp.exp(s - mx).astype(BF16)
        den = _dot(p, ones) + jnp.exp(sk - mx)
        o = _dot(p, vv) / den
        if half == 0:
            acc = o
        else:
            o_ref[:, hp * pair:(hp + 1) * pair] = (acc + o).astype(o_ref.dtype)


def swa_attention(z, bias_tab, sinks):
    m = z.shape[0]
    nb = SEQ // BLOCK
    kcol = Q_WIDTH // KV_WIDTH
    return pl.pallas_call(
        _attn_kernel,
        grid=(BATCH, nb),
        in_specs=[pl.BlockSpec(memory_space=pltpu.SMEM),
                  pl.BlockSpec((BLOCK, Q_WIDTH), lambda b, n: (b * nb + n, 0)),
                  pl.BlockSpec((BLOCK, KV_WIDTH), lambda b, n: (b * nb + jnp.maximum(n - 1, 0), kcol)),
                  pl.BlockSpec((BLOCK, KV_WIDTH), lambda b, n: (b * nb + n, kcol)),
                  pl.BlockSpec((BLOCK, KV_WIDTH), lambda b, n: (b * nb + jnp.maximum(n - 1, 0), kcol + 1)),
                  pl.BlockSpec((BLOCK, KV_WIDTH), lambda b, n: (b * nb + n, kcol + 1)),
                  pl.BlockSpec((None, N_Q_HEADS, BLOCK, 2 * BLOCK),
                               lambda b, n: (jnp.minimum(n, 1), 0, 0, 0))],
        out_specs=pl.BlockSpec((BLOCK, Q_WIDTH), lambda b, n: (b * nb + n, 0)),
        out_shape=jax.ShapeDtypeStruct((m, Q_WIDTH), BF16),
        compiler_params=_cparams(("arbitrary", "arbitrary")),
        name="swa_attention",
    )(sinks, z, z, z, z, z, bias_tab)


def _ssm_prep_kernel(are_l_ref, aim_l_ref, ls_l_ref, are_s_ref, aim_s_ref, ls_s_ref,
                     btr_ref, bti_ref, ctr_ref, cti_ref,
                     kt_ref, ftr_ref, fti_ref, etr_ref, eti_ref, lr_ref, li_ref, *, scan_steps):
    are = are_l_ref[...]
    aim = aim_l_ref[...]
    dt = jnp.exp(ls_l_ref[...])
    lre = are * dt
    th = aim * dt
    mag = jnp.exp(lre)
    lam_r = mag * jnp.cos(th)
    lam_i = mag * jnp.sin(th)
    den = are * are + aim * aim
    n_r = lam_r - 1.0
    n_i = lam_i
    f_r = (n_r * are + n_i * aim) / den
    f_i = (n_i * are - n_r * aim) / den
    btr = btr_ref[...]
    bti = bti_ref[...]
    bb_r = f_r * btr - f_i * bti
    bb_i = f_r * bti + f_i * btr
    kk = (CHUNK - 1 - lax.broadcasted_iota(jnp.int32, (CHUNK, SSM_STATE), 0)).astype(F32)
    pmag = jnp.exp(kk * lre)
    pw_r = pmag * jnp.cos(kk * th)
    pw_i = pmag * jnp.sin(kk * th)
    for s in range(CHUNK):
        pr = pw_r[s:s + 1, :]
        pi = pw_i[s:s + 1, :]
        ftr_ref[s * SSM_GROUP_CH:(s + 1) * SSM_GROUP_CH, :] = bb_r * pr - bb_i * pi
        fti_ref[s * SSM_GROUP_CH:(s + 1) * SSM_GROUP_CH, :] = bb_r * pi + bb_i * pr
    cmag = jnp.exp(float(CHUNK) * lre)
    cr = cmag * jnp.cos(float(CHUNK) * th)
    ci = cmag * jnp.sin(float(CHUNK) * th)
    for k in range(scan_steps):
        lr_ref[k:k + 1, :] = cr
        li_ref[k:k + 1, :] = ci
        cr, ci = cr * cr - ci * ci, 2.0 * cr * ci
    are_s = are_s_ref[...]
    aim_s = aim_s_ref[...]
    dt_s = jnp.exp(ls_s_ref[...])
    lre_s = are_s * dt_s
    th_s = aim_s * dt_s
    kl = (lax.broadcasted_iota(jnp.int32, (1, CHUNK_W), 1) // SSM_GROUP_CH).astype(F32)
    ctr = ctr_ref[...]
    cti = cti_ref[...]

    def c_times_power(k0):
        e = kl + k0
        mg = jnp.exp(e * lre_s)
        pr = mg * jnp.cos(e * th_s)
        pi = mg * jnp.sin(e * th_s)
        return ctr * pr - cti * pi, ctr * pi + cti * pr

    w0_r, w0_i = c_times_power(0.0)
    hi = lax.Precision.HIGHEST
    kt_ref[...] = (jnp.dot(bb_r, w0_r, precision=hi, preferred_element_type=F32)
                   - jnp.dot(bb_i, w0_i, precision=hi, preferred_element_type=F32))
    w1_r, w1_i = c_times_power(1.0)
    etr_ref[...] = w1_r
    eti_ref[...] = -w1_i


def ssm_prep(a_re, a_im, log_step, b_re, b_im, c_re, c_im, scan_steps):
    g, p, h = SSM_GROUPS, SSM_STATE, SSM_GROUP_CH
    ls = log_step.astype(F32)
    lane = lambda a: a.astype(F32).reshape(g, 1, p)
    sub = lambda a: a.astype(F32).reshape(g, p, 1)
    ls_l = jnp.broadcast_to(ls[:, None, None], (g, 1, p))
    ls_s = jnp.broadcast_to(ls[:, None, None], (g, p, 1))
    bt = lambda b: jnp.swapaxes(b.astype(F32), 1, 2)
    ct = lambda c: jnp.tile(jnp.swapaxes(c.astype(F32), 1, 2), (1, 1, CHUNK))
    spec3 = lambda s1, s2: pl.BlockSpec((None, s1, s2), lambda i: (i, 0, 0))
    outs = pl.pallas_call(
        functools.partial(_ssm_prep_kernel, scan_steps=scan_steps),
        grid=(g,),
        in_specs=[spec3(1, p), spec3(1, p), spec3(1, p), spec3(p, 1), spec3(p, 1), spec3(p, 1),
                  spec3(h, p), spec3(h, p), spec3(p, CHUNK_W), spec3(p, CHUNK_W)],
        out_specs=[spec3(h, CHUNK_W), spec3(CHUNK_W, p), spec3(CHUNK_W, p),
                   spec3(p, CHUNK_W), spec3(p, CHUNK_W), spec3(scan_steps, p), spec3(scan_steps, p)],
        out_shape=[jax.ShapeDtypeStruct((g, h, CHUNK_W), F32),
                   jax.ShapeDtypeStruct((g, CHUNK_W, p), F32),
                   jax.ShapeDtypeStruct((g, CHUNK_W, p), F32),
                   jax.ShapeDtypeStruct((g, p, CHUNK_W), F32),
                   jax.ShapeDtypeStruct((g, p, CHUNK_W), F32),
                   jax.ShapeDtypeStruct((g, scan_steps, p), F32),
                   jax.ShapeDtypeStruct((g, scan_steps, p), F32)],
        compiler_params=_cparams(("arbitrary",)),
        name="ssm_prep",
    )(lane(a_re), lane(a_im), ls_l, sub(a_re), sub(a_im), ls_s,
      bt(b_re), bt(b_im), ct(c_re), ct(c_im))
    kt, ftr, fti, etr, eti, lr, li = outs
    gl = SSM_LANE_GROUPS
    nb = g // gl
    p2 = 2 * p
    same = jnp.eye(gl, dtype=bool)
    kt5 = kt.reshape(nb, gl, h, CHUNK, h)[:, :, :, ::-1, :]
    kst = jnp.where(same[None, None, :, None, :, None],
                    kt5.transpose(0, 3, 1, 2, 4)[:, :, :, :, None, :], 0.0)
    kst = kst.reshape(nb, CHUNK * gl * h, gl * h)
    ft5 = jnp.concatenate([ftr, fti], axis=2).reshape(nb, gl, CHUNK, h, p2)
    fb = jnp.where(same[None, None, :, None, :, None],
                   ft5.transpose(0, 2, 1, 3, 4)[:, :, :, :, None, :], 0.0)
    fb = fb.reshape(nb, CHUNK * gl * h, gl * p2)
    et5 = jnp.concatenate([etr, eti], axis=1).reshape(nb, gl, p2, CHUNK, h)
    eb = jnp.where(same[None, :, None, None, :, None], et5[:, :, :, :, None, :], 0.0)
    eb = eb.reshape(nb, gl * p2, CHUNK * gl * h)
    lanes = lambda a: a.reshape(nb, gl, scan_steps, p2).transpose(0, 2, 1, 3).reshape(nb, scan_steps, gl * p2)
    l_same = lanes(jnp.concatenate([lr, lr], axis=2))
    l_swap = lanes(jnp.concatenate([-li, li], axis=2))
    return kst.astype(BF16), fb.astype(BF16), eb.astype(BF16), l_same, l_swap


def _ssm_kernel(z_ref, kst_ref, fb_ref, eb_ref, ls_ref, lw_ref, d_ref, y_ref, u32_ref, s_ref,
                *, scan_steps):
    rows = z_ref.shape[0] // CHUNK
    lanes = z_ref.shape[1]
    p2 = 2 * SSM_STATE
    u32_ref[...] = z_ref[...].astype(F32)

    def tokens(t):
        return u32_ref[pl.ds(t, rows, stride=CHUNK), :]

    u_all = jnp.concatenate([tokens(t).astype(BF16) for t in range(CHUNK)], axis=1)
    s_ref[...] = _dot(u_all, fb_ref[...])
    intra = [_dot(u_all[:, :(t + 1) * lanes], kst_ref[(CHUNK - 1 - t) * lanes:, :]) for t in range(CHUNK)]
    ridx = lax.broadcasted_iota(jnp.int32, (rows, 1), 0)
    prevs = []
    for g in range(SSM_LANE_GROUPS):
        cols = slice(g * p2, (g + 1) * p2)
        x = s_ref[:, cols]
        for k in range(scan_steps):
            d = 1 << k
            sh = jnp.where(ridx >= d, pltpu.roll(x, d, 0), 0.0)
            x = x + ls_ref[k:k + 1, cols] * sh + lw_ref[k:k + 1, cols] * pltpu.roll(sh, SSM_STATE, 1)
        prevs.append(jnp.where(ridx >= 1, pltpu.roll(x, 1, 0), 0.0).astype(BF16))
    y_state = _dot(jnp.concatenate(prevs, axis=1), eb_ref[...])
    for t in range(CHUNK):
        y = y_state[:, t * lanes:(t + 1) * lanes] + intra[t] + d_ref[...] * tokens(t)
        y_ref[pl.ds(t, rows, stride=CHUNK), :] = y


def ssm_chunked(z, kst, fb, eb, l_same, l_swap, d_skip, scan_steps):
    m = z.shape[0]
    nb = kst.shape[0]
    lanes = SSM_LANE_GROUPS * SSM_GROUP_CH
    ucol = (Q_WIDTH + 2 * KV_WIDTH) // lanes
    wspec = lambda a: pl.BlockSpec((None,) + a.shape[1:], lambda i, b: (i, 0, 0))
    return pl.pallas_call(
        functools.partial(_ssm_kernel, scan_steps=scan_steps),
        grid=(nb, m // SEQ),
        in_specs=[pl.BlockSpec((SEQ, lanes), lambda i, b: (b, ucol + i)),
                  wspec(kst), wspec(fb), wspec(eb), wspec(l_same), wspec(l_swap),
                  pl.BlockSpec((1, lanes), lambda i, b: (0, i))],
        out_specs=pl.BlockSpec((SEQ, lanes), lambda i, b: (b, i)),
        out_shape=jax.ShapeDtypeStruct((m, SSM_WIDTH), F32),
        scratch_shapes=[pltpu.VMEM((SEQ, lanes), F32),
                        pltpu.VMEM((SEQ // CHUNK, SSM_LANE_GROUPS * 2 * SSM_STATE), F32)],
        compiler_params=_cparams(("arbitrary", "arbitrary")),
        name="ssm_chunked",
    )(z, kst, fb, eb, l_same, l_swap, d_skip.astype(F32).reshape(1, SSM_WIDTH))


def _glu_kernel(y_ref, w_ref, b_ref, o_ref):
    y = y_ref[...]
    gel = 0.5 * y * (1.0 + jnp.tanh(math.sqrt(2.0 / math.pi) * (y + 0.044715 * (y * y * y))))
    t = _dot(gel.astype(BF16), w_ref[...]) + b_ref[...]
    o_ref[...] = (gel * (1.0 / (1.0 + jnp.exp(-t)))).astype(o_ref.dtype)


def gelu_glu(y, w_glu, b_glu):
    m = y.shape[0]
    tm = min(TM_GLU, m)
    return pl.pallas_call(
        _glu_kernel,
        grid=(m // tm,),
        in_specs=[pl.BlockSpec((tm, SSM_WIDTH), lambda i: (i, 0)),
                  pl.BlockSpec((SSM_WIDTH, SSM_WIDTH), lambda i: (0, 0)),
                  pl.BlockSpec((1, SSM_WIDTH), lambda i: (0, 0))],
        out_specs=pl.BlockSpec((tm, SSM_WIDTH), lambda i: (i, 0)),
        out_shape=jax.ShapeDtypeStruct((m, SSM_WIDTH), BF16),
        compiler_params=_cparams(("arbitrary",)),
        name="gelu_glu",
    )(y, w_glu, b_glu.reshape(1, SSM_WIDTH))


def _t5_bucket_table():
    qi = jnp.arange(BLOCK)[:, None]
    kj = jnp.arange(2 * BLOCK)[None, :]
    n = jnp.maximum(qi + BLOCK - kj, 0)
    max_exact = REL_BUCKETS // 2
    nf = jnp.maximum(n, 1).astype(F32)
    large = max_exact + (jnp.log(nf / max_exact) / math.log(REL_MAX_DIST / max_exact)
                         * (REL_BUCKETS - max_exact)).astype(jnp.int32)
    large = jnp.minimum(large, REL_BUCKETS - 1)
    return jnp.where(n < max_exact, n, large).astype(jnp.int32)


def _even_mixer(h, w_in, pool_w, pool_scale, conv_w):
    return [pool_mixer(h, w_in, pool_w, pool_scale), sconv_mixer(h, w_in, conv_w)]


def _odd_mixer(h, w_in, sinks, bias_tab, a_re, a_im, log_step, b_re, b_im, c_re, c_im,
               d_skip, w_glu, b_glu):
    scan_steps = max(1, (SEQ // CHUNK - 1).bit_length())
    z = matmul_bf16(h, w_in)
    y_attn = swa_attention(z, bias_tab, sinks)
    kst, fb, eb, l_same, l_swap = ssm_prep(a_re, a_im, log_step, b_re, b_im, c_re, c_im, scan_steps)
    y = ssm_chunked(z, kst, fb, eb, l_same, l_swap, d_skip, scan_steps)
    return [y_attn, gelu_glu(y, w_glu, b_glu)]


def kernel(x, rel_bias, norm_mix_pre, norm_mix_post, norm_ffn_pre, norm_ffn_post, e_w_in, e_pool_w, e_pool_scale, e_conv_w, e_w_out, o_w_in, o_sinks, o_a_re, o_a_im, o_log_step, o_b_re, o_b_im, o_c_re, o_c_im, o_d, o_glu_w, o_glu_b, o_w_out, f_w_up, f_conv_w, f_w_down):
    bf = lambda a: a.astype(BF16)
    xr = x.reshape(BATCH * SEQ, D_MODEL).astype(F32)
    bias_tab = rel_bias_table(rel_bias.astype(F32), _t5_bucket_table())
    h = rms_norm_bf16(xr, norm_mix_pre[0].astype(F32))
    w_up_all, w_down_all, fconv_all = bf(f_w_up), bf(f_w_down), f_conv_w.astype(F32)
    e_out_all, o_out_all = bf(e_w_out), bf(o_w_out)
    for i in range(DEPTH):
        j = i // 2
        if i % 2 == 0:
            y = _even_mixer(h, bf(e_w_in[j]), bf(e_pool_w[j]), e_pool_scale[j].astype(F32),
                            e_conv_w[j].astype(F32))
            w_out = e_out_all
        else:
            y = _odd_mixer(h, bf(o_w_in[j]), o_sinks[j].astype(F32), bias_tab,
                           o_a_re[j], o_a_im[j], o_log_step[j], o_b_re[j], o_b_im[j],
                           o_c_re[j], o_c_im[j], o_d[j], bf(o_glu_w[j]), o_glu_b[j].astype(F32))
            w_out = o_out_all
        xr, h = outproj_residual_norm(y, w_out, j, xr, norm_mix_post[i].astype(F32),
                                      norm_ffn_pre[i].astype(F32))
        act = ffn_up(h, w_up_all, fconv_all, i)
        g_next = norm_mix_pre[i + 1].astype(F32) if i + 1 < DEPTH else None
        xr, h = outproj_residual_norm([act], w_down_all, i, xr, norm_ffn_post[i].astype(F32), g_next)
    return xr.reshape(x.shape)
```

```python
import functools
import math

import jax
import jax.numpy as jnp
from jax import lax
from jax.experimental import pallas as pl
from jax.experimental.pallas import tpu as pltpu

F32 = jnp.float32
BF16 = jnp.bfloat16

D_MODEL = 4096
BATCH = 4
SEQ = 4096
DEPTH = 4

POOL_WIDTH = D_MODEL // 2
POOL_WINDOWS = (2, 4, 8, 16)
POOL_GROUP = POOL_WIDTH // len(POOL_WINDOWS)
SCONV_WIDTH = D_MODEL // 2
HEAD_DIM = 64
N_Q_HEADS = D_MODEL // 2 // HEAD_DIM
N_KV_HEADS = N_Q_HEADS // 4
Q_PER_KV = N_Q_HEADS // N_KV_HEADS
WINDOW = 128
BLOCK = 128
Q_WIDTH = N_Q_HEADS * HEAD_DIM
KV_WIDTH = N_KV_HEADS * HEAD_DIM
SSM_WIDTH = D_MODEL // 4
SSM_GROUP_CH = 16
SSM_GROUPS = SSM_WIDTH // SSM_GROUP_CH
SSM_STATE = 64
ODD_OUT = Q_WIDTH + SSM_WIDTH
REL_BUCKETS = 32
REL_MAX_DIST = 128
D_FF = ((8 * D_MODEL // 3 + 255) // 256) * 256
EPS = 1e-6

CHUNK = 16
CHUNK_W = CHUNK * SSM_GROUP_CH

HALO = 16
CONV_HALO = 8
NEG = -1e30

TM_NORM = 256
TM_MIX = 512
TM_OUT = 512
TK_OUT = 1024
TM_FFN = 1024
FFN_TILES = 2
FFN_ROW_SPLIT = (7, 8)
TM_MM = 1024
TM_GLU = 512
TN_FFN = 256
ATTN_LOOKAHEAD = 5
SSM_LANE_GROUPS = 8

VMEM_LIMIT = 56 * 1024 * 1024
VMEM_LIMIT_OUTPROJ = 62 * 1024 * 1024


def _cparams(sem, vmem=VMEM_LIMIT):
    return pltpu.CompilerParams(dimension_semantics=sem, vmem_limit_bytes=vmem)


def _dot(a, b):
    return jnp.dot(a, b, preferred_element_type=F32)


def _load_history(ext_view, carry_view, is_first, rows):
    @pl.when(is_first)
    def _():
        ext_view[0:rows, :] = jnp.zeros((rows, ext_view.shape[1]), F32)

    @pl.when(jnp.logical_not(is_first))
    def _():
        ext_view[0:rows, :] = carry_view[...]


def _rms_kernel(x_ref, g_ref, o_ref):
    x = x_ref[...]
    r = lax.rsqrt(jnp.mean(x * x, axis=-1, keepdims=True) + EPS)
    o_ref[...] = (x * r * g_ref[...]).astype(o_ref.dtype)


def rms_norm_bf16(x, g):
    m, d = x.shape
    tm = min(TM_NORM, m)
    return pl.pallas_call(
        _rms_kernel,
        grid=(m // tm,),
        in_specs=[pl.BlockSpec((tm, d), lambda i: (i, 0)),
                  pl.BlockSpec((1, d), lambda i: (0, 0))],
        out_specs=pl.BlockSpec((tm, d), lambda i: (i, 0)),
        out_shape=jax.ShapeDtypeStruct((m, d), BF16),
        compiler_params=_cparams(("arbitrary",)),
        name="rms_norm",
    )(x, g.reshape(1, d))


def _pool_kernel(h_ref, w_ref, pw_ref, sc_ref, o_ref, carry_ref, ext_ref, *, tm, tiles_per_seq):
    i = pl.program_id(0)
    j = pl.program_id(1)
    seq_tile = i % tiles_per_seq
    _load_history(ext_ref, carry_ref.at[j], seq_tile == 0, HALO)
    half = tm // 2
    for gi, w in enumerate(POOL_WINDOWS):
        @pl.when(j == gi)
        def _():
            for r0 in (0, half):
                ext_ref[HALO + r0:HALO + r0 + half, :] = _dot(h_ref[r0:r0 + half, :], w_ref[...])
            for r0 in (0, half):
                cur = ext_ref[HALO + r0:HALO + r0 + half, :]
                s = cur
                for d in range(1, w):
                    s = s + ext_ref[HALO + r0 - d:HALO + r0 - d + half, :]
                t = seq_tile * tm + r0 + lax.broadcasted_iota(jnp.int32, (half, 1), 0)
                den = jnp.minimum(t + 1, w).astype(F32)
                p = s / den - cur
                y = _dot(p.astype(BF16), pw_ref[...]) * sc_ref[...]
                o_ref[r0:r0 + half, :] = y.astype(o_ref.dtype)
    carry_ref[j] = ext_ref[tm:tm + HALO, :]


def pool_mixer(h, w_in, pool_w, pool_scale):
    m, d = h.shape
    tn = POOL_GROUP
    tm = min(TM_MIX, SEQ)
    ng = len(POOL_WINDOWS)
    return pl.pallas_call(
        functools.partial(_pool_kernel, tm=tm, tiles_per_seq=SEQ // tm),
        grid=(m // tm, ng),
        in_specs=[pl.BlockSpec((tm, d), lambda i, j: (i, 0)),
                  pl.BlockSpec((d, tn), lambda i, j: (0, j)),
                  pl.BlockSpec((None, tn, tn), lambda i, j: (j, 0, 0)),
                  pl.BlockSpec((1, tn), lambda i, j: (0, j))],
        out_specs=pl.BlockSpec((tm, tn), lambda i, j: (i, j)),
        out_shape=jax.ShapeDtypeStruct((m, POOL_WIDTH), BF16),
        scratch_shapes=[pltpu.VMEM((ng, HALO, tn), F32),
                        pltpu.VMEM((tm + HALO, tn), F32)],
        compiler_params=_cparams(("arbitrary", "arbitrary")),
        name="pool_mixer",
    )(h, w_in, pool_w, pool_scale.reshape(1, POOL_WIDTH))


def _sconv_kernel(h_ref, whb_ref, wgb_ref, wgc_ref, cw_ref, o_ref, carry_ref, ext_ref,
                  *, tm, tiles_per_seq):
    i = pl.program_id(0)
    j = pl.program_id(1)
    _load_history(ext_ref, carry_ref.at[j], i % tiles_per_seq == 0, CONV_HALO)
    h = h_ref[...]
    u = _dot(h, wgc_ref[...]) * _dot(h, whb_ref[...])
    ext_ref[CONV_HALO:, :] = u
    carry_ref[j] = u[tm - CONV_HALO:, :]
    cw = cw_ref[...]
    conv = (cw[2:3, :] * ext_ref[CONV_HALO:, :]
            + cw[1:2, :] * ext_ref[CONV_HALO - 1:CONV_HALO - 1 + tm, :]
            + cw[0:1, :] * ext_ref[CONV_HALO - 2:CONV_HALO - 2 + tm, :])
    o_ref[...] = (_dot(h, wgb_ref[...]) * conv).astype(o_ref.dtype)


def sconv_mixer(h, w_in, conv_w):
    m, d = h.shape
    tn = min(512, SCONV_WIDTH)
    tm = min(TM_MIX, SEQ)
    nb = SCONV_WIDTH // tn
    off = POOL_WIDTH // tn
    return pl.pallas_call(
        functools.partial(_sconv_kernel, tm=tm, tiles_per_seq=SEQ // tm),
        grid=(m // tm, nb),
        in_specs=[pl.BlockSpec((tm, d), lambda i, j: (i, 0)),
                  pl.BlockSpec((d, tn), lambda i, j: (0, off + j)),
                  pl.BlockSpec((d, tn), lambda i, j: (0, off + nb + j)),
                  pl.BlockSpec((d, tn), lambda i, j: (0, off + 2 * nb + j)),
                  pl.BlockSpec((3, tn), lambda i, j: (0, j))],
        out_specs=pl.BlockSpec((tm, tn), lambda i, j: (i, j)),
        out_shape=jax.ShapeDtypeStruct((m, SCONV_WIDTH), BF16),
        scratch_shapes=[pltpu.VMEM((nb, CONV_HALO, tn), F32),
                        pltpu.VMEM((tm + CONV_HALO, tn), F32)],
        compiler_params=_cparams(("arbitrary", "arbitrary")),
        name="sconv_mixer",
    )(h, w_in, w_in, w_in, conv_w)


def _outproj_kernel(*refs, nks, tail_rows, emit_h):
    ns = len(nks)
    y_refs = refs[:ns]
    w_ref, x_hbm, gpost_ref, gpre_ref, xo_ref = refs[ns:ns + 5]
    rest = refs[ns + 5:]
    ho_ref = rest[0] if emit_h else None
    x_ref, x_sem = rest[-2:]
    nk = sum(nks)
    assert nk >= 2
    tm = xo_ref.shape[0]
    i = pl.program_id(0)
    k = pl.program_id(1)

    x_copy = pltpu.make_async_copy(x_hbm.at[pl.ds(i * tm, tm), :], x_ref, x_sem)

    def accumulate(y_ref, masked, first):
        w = w_ref[...]
        if masked:
            row = lax.broadcasted_iota(jnp.int32, (w.shape[0], 1), 0)
            w = jnp.where(row < tail_rows, w, jnp.zeros_like(w))
        if first:
            xo_ref[...] = _dot(y_ref[...], w)
        else:
            xo_ref[...] += _dot(y_ref[...], w)

    @pl.when(k == 0)
    def _():
        x_copy.start()
        accumulate(y_refs[0], False, True)

    k0 = 0
    for s, nk_s in enumerate(nks):
        has_tail = tail_rows is not None and s == ns - 1
        lo = max(k0, 1)
        hi = k0 + nk_s - (1 if has_tail else 0)
        if hi > lo:
            pl.when((k >= lo) & (k < hi))(functools.partial(accumulate, y_refs[s], False, False))
        if has_tail:
            pl.when(k == nk - 1)(functools.partial(accumulate, y_refs[s], True, False))
        k0 += nk_s

    @pl.when(k == nk - 1)
    def _():
        x_copy.wait()
        mix = xo_ref[...]
        r = lax.rsqrt(jnp.mean(mix * mix, axis=-1, keepdims=True) + EPS)
        xn = x_ref[...] + mix * r * gpost_ref[...]
        xo_ref[...] = xn
        if emit_h:
            r2 = lax.rsqrt(jnp.mean(xn * xn, axis=-1, keepdims=True) + EPS)
            ho_ref[...] = (xn * r2 * gpre_ref[...]).astype(ho_ref.dtype)


def outproj_residual_norm(ys, w, layer, x, g_post, g_pre_next):
    m = x.shape[0]
    _, krows, d = w.shape
    tm = min(TM_OUT, m)
    tk = next(t for t in (TK_OUT, 512, 256) if all(y.shape[1] % t == 0 for y in ys))
    nks = tuple(y.shape[1] // tk for y in ys)
    nk = sum(nks)
    tail = krows - (nk - 1) * tk
    assert 0 < tail <= tk
    emit_h = g_pre_next is not None
    if not emit_h:
        g_pre_next = g_post
    out_shape = [jax.ShapeDtypeStruct((m, d), F32)]
    out_specs = [pl.BlockSpec((tm, d), lambda i, k: (i, 0))]
    if emit_h:
        out_shape.append(jax.ShapeDtypeStruct((m, d), BF16))
        out_specs.append(pl.BlockSpec((tm, d), lambda i, k: (i, 0)))
    y_specs = []
    k0 = 0
    for nk_s in nks:
        y_specs.append(pl.BlockSpec(
            (tm, tk), functools.partial(lambda i, k, k0, n: (i, jnp.clip(k - k0, 0, n - 1)), k0=k0, n=nk_s)))
        k0 += nk_s
    res = pl.pallas_call(
        functools.partial(_outproj_kernel, nks=nks, tail_rows=None if tail == tk else tail, emit_h=emit_h),
        grid=(m // tm, nk),
        in_specs=y_specs + [
            pl.BlockSpec((None, tk, d), lambda i, k: (layer, k, 0)),
            pl.BlockSpec(memory_space=pl.ANY),
            pl.BlockSpec((1, d), lambda i, k: (0, 0)),
            pl.BlockSpec((1, d), lambda i, k: (0, 0))],
        out_specs=out_specs,
        out_shape=out_shape,
        scratch_shapes=[pltpu.VMEM((tm, d), F32), pltpu.SemaphoreType.DMA(())],
        compiler_params=_cparams(("arbitrary", "arbitrary"), VMEM_LIMIT_OUTPROJ),
        name="outproj_norm",
    )(*ys, w, x, g_post.reshape(1, d), g_pre_next.reshape(1, d))
    return (res[0], res[1]) if emit_h else (res[0], None)


def _ffn_pad():
    step = FFN_TILES * TN_FFN
    return -(-D_FF // step) * step


def _ffn_up_kernel(h_ref, *refs, tm, tn, tiles, n_valid, tiles_per_seq):
    w_refs = refs[:2 * tiles]
    c_refs = refs[2 * tiles:4 * tiles]
    o_ref, carry_ref, ext_ref = refs[4 * tiles:]
    i = pl.program_id(0)
    j = pl.program_id(1)
    _load_history(ext_ref, carry_ref.at[j], i % tiles_per_seq == 0, CONV_HALO)
    bounds = [0] + [tm * f // FFN_ROW_SPLIT[-1] for f in FFN_ROW_SPLIT]
    for r0, r1 in zip(bounds[:-1], bounds[1:]):
        rows = r1 - r0
        lo = CONV_HALO + r0
        hc = h_ref[r0:r1, :]
        conv = []
        for idx in range(2 * tiles):
            col = idx * tn
            z = _dot(hc, w_refs[idx][...])
            ext_ref[lo:lo + rows, col:col + tn] = z
            cw = c_refs[idx][...]
            conv.append(cw[2:3, :] * z
                        + cw[1:2, :] * ext_ref[lo - 1:lo - 1 + rows, col:col + tn]
                        + cw[0:1, :] * ext_ref[lo - 2:lo - 2 + rows, col:col + tn])
        for t in range(tiles):
            gate, up = conv[t], conv[tiles + t]
            act = gate * (1.0 / (1.0 + jnp.exp(-gate))) * up
            act = jnp.where(j * tiles + t < n_valid, act, 0.0)
            o_ref[r0:r1, t * tn:(t + 1) * tn] = act.astype(o_ref.dtype)
    carry_ref[j] = ext_ref[tm:tm + CONV_HALO, :]


def ffn_up(h, w_up, conv_w, layer):
    m, d = h.shape
    tn = TN_FFN
    tiles = FFN_TILES
    tm = min(TM_FFN, SEQ)
    n_valid = D_FF // tn
    nsteps = _ffn_pad() // (tiles * tn)

    def col(off, t):
        return lambda i, j: (layer, 0, off + jnp.minimum(j * tiles + t, n_valid - 1))

    offsets = [(0, t) for t in range(tiles)] + [(n_valid, t) for t in range(tiles)]
    return pl.pallas_call(
        functools.partial(_ffn_up_kernel, tm=tm, tn=tn, tiles=tiles, n_valid=n_valid,
                          tiles_per_seq=SEQ // tm),
        grid=(m // tm, nsteps),
        in_specs=([pl.BlockSpec((tm, d), lambda i, j: (i, 0))]
                  + [pl.BlockSpec((None, d, tn), col(off, t)) for off, t in offsets]
                  + [pl.BlockSpec((None, 3, tn), col(off, t)) for off, t in offsets]),
        out_specs=pl.BlockSpec((tm, tiles * tn), lambda i, j: (i, j)),
        out_shape=jax.ShapeDtypeStruct((m, nsteps * tiles * tn), BF16),
        scratch_shapes=[pltpu.VMEM((nsteps, CONV_HALO, 2 * tiles * tn), F32),
                        pltpu.VMEM((tm + CONV_HALO, 2 * tiles * tn), F32)],
        compiler_params=_cparams(("arbitrary", "arbitrary")),
        name="ffn_up",
    )(h, *([w_up] * (2 * tiles)), *([conv_w] * (2 * tiles)))


def _mm_kernel(a_ref, w_ref, o_ref):
    o_ref[...] = _dot(a_ref[...], w_ref[...]).astype(o_ref.dtype)


def matmul_bf16(a, w):
    m, kdim = a.shape
    n = w.shape[1]
    tm = min(TM_MM, m)
    tn = min(512, n)
    return pl.pallas_call(
        _mm_kernel,
        grid=(m // tm, n // tn),
        in_specs=[pl.BlockSpec((tm, kdim), lambda i, j: (i, 0)),
                  pl.BlockSpec((kdim, tn), lambda i, j: (0, j))],
        out_specs=pl.BlockSpec((tm, tn), lambda i, j: (i, j)),
        out_shape=jax.ShapeDtypeStruct((m, n), BF16),
        compiler_params=_cparams(("arbitrary", "arbitrary")),
        name="matmul",
    )(a, w)


def _bias_kernel(rb_ref, bucket_ref, o_ref):
    variant = pl.program_id(0)
    h = pl.program_id(1)
    bucket = bucket_ref[...]
    acc = jnp.zeros(bucket.shape, F32)
    for b in range(REL_BUCKETS):
        acc = jnp.where(bucket == b, rb_ref[b, h], acc)
    qi = lax.broadcasted_iota(jnp.int32, bucket.shape, 0)
    kj = lax.broadcasted_iota(jnp.int32, bucket.shape, 1)
    dist = qi + BLOCK - kj
    valid = (dist >= 0) & (dist < WINDOW) & ((kj >= BLOCK) | (variant > 0))
    o_ref[...] = jnp.where(valid, acc, NEG)


def rel_bias_table(rel_bias, bucket):
    return pl.pallas_call(
        _bias_kernel,
        grid=(2, N_Q_HEADS),
        in_specs=[pl.BlockSpec(memory_space=pltpu.SMEM),
                  pl.BlockSpec((BLOCK, 2 * BLOCK), lambda v, h: (0, 0))],
        out_specs=pl.BlockSpec((None, None, BLOCK, 2 * BLOCK), lambda v, h: (v, h, 0, 0)),
        out_shape=jax.ShapeDtypeStruct((2, N_Q_HEADS, BLOCK, 2 * BLOCK), F32),
        compiler_params=_cparams(("arbitrary", "arbitrary")),
        name="rel_bias_table",
    )(rel_bias, bucket)


def _attn_kernel(sink_ref, q_ref, kp_ref, kc_ref, vp_ref, vc_ref, bias_ref, o_ref):
    k = jnp.concatenate([kp_ref[...], kc_ref[...]], axis=0)
    v = jnp.concatenate([vp_ref[...], vc_ref[...]], axis=0)
    scale = HEAD_DIM ** -0.5
    pair = 2 * HEAD_DIM
    low = lax.broadcasted_iota(jnp.int32, (1, pair), 1) < HEAD_DIM
    zero = jnp.zeros((), BF16)
    ones = jnp.ones((2 * BLOCK, pair), BF16)

    def swap_halves(x):
        return pltpu.bitcast(pltpu.roll(pltpu.bitcast(x, jnp.int32), HEAD_DIM, 1), BF16)

    kv = {}
    for gp in range(N_KV_HEADS // 2):
        kblk = k[:, gp * pair:(gp + 1) * pair]
        vblk = v[:, gp * pair:(gp + 1) * pair]
        kswp = swap_halves(kblk)
        vswp = swap_halves(vblk)
        for e in range(2):
            k_lo, k_hi = (kblk, kswp) if e == 0 else (kswp, kblk)
            v_lo = jnp.where(low, vblk if e == 0 else vswp, zero)
            v_hi = jnp.where(low, zero, vswp if e == 0 else vblk)
            kv[2 * gp + e] = ((k_lo, v_lo), (k_hi, v_hi))

    def scores(h):
        hp, half = divmod(h, 2)
        qp = q_ref[:, hp * pair:(hp + 1) * pair] * scale
        qm = jnp.where(low, qp, zero) if half == 0 else jnp.where(low, zero, qp)
        kk = kv[h // Q_PER_KV][half][0]
        return lax.dot_general(qm, kk, (((1,), (1,)), ((), ())), preferred_element_type=F32) + bias_ref[h]

    pending = [scores(h) for h in range(ATTN_LOOKAHEAD)]
    acc = None
    for h in range(N_Q_HEADS):
        s = pending.pop(0)
        if h + ATTN_LOOKAHEAD < N_Q_HEADS:
            pending.append(scores(h + ATTN_LOOKAHEAD))
        hp, half = divmod(h, 2)
        vv = kv[h // Q_PER_KV][half][1]
        sk = sink_ref[h]
        mx = jnp.maximum(jnp.max(s, axis=-1, keepdims=True), sk)
        p = jnp.exp(s - mx).astype(BF16)
        den = _dot(p, ones) + jnp.exp(sk - mx)
        o = _dot(p, vv) / den
        if half == 0:
            acc = o
        else:
            o_ref[:, hp * pair:(hp + 1) * pair] = (acc + o).astype(o_ref.dtype)


def swa_attention(z, bias_tab, sinks):
    m = z.shape[0]
    nb = SEQ // BLOCK
    kcol = Q_WIDTH // KV_WIDTH
    return pl.pallas_call(
        _attn_kernel,
        grid=(BATCH, nb),
        in_specs=[pl.BlockSpec(memory_space=pltpu.SMEM),
                  pl.BlockSpec((BLOCK, Q_WIDTH), lambda b, n: (b * nb + n, 0)),
                  pl.BlockSpec((BLOCK, KV_WIDTH), lambda b, n: (b * nb + jnp.maximum(n - 1, 0), kcol)),
                  pl.BlockSpec((BLOCK, KV_WIDTH), lambda b, n: (b * nb + n, kcol)),
                  pl.BlockSpec((BLOCK, KV_WIDTH), lambda b, n: (b * nb + jnp.maximum(n - 1, 0), kcol + 1)),
                  pl.BlockSpec((BLOCK, KV_WIDTH), lambda b, n: (b * nb + n, kcol + 1)),
                  pl.BlockSpec((None, N_Q_HEADS, BLOCK, 2 * BLOCK),
                               lambda b, n: (jnp.minimum(n, 1), 0, 0, 0))],
        out_specs=pl.BlockSpec((BLOCK, Q_WIDTH), lambda b, n: (b * nb + n, 0)),
        out_shape=jax.ShapeDtypeStruct((m, Q_WIDTH), BF16),
        compiler_params=_cparams(("arbitrary", "arbitrary")),
        name="swa_attention",
    )(sinks, z, z, z, z, z, bias_tab)


def _ssm_prep_kernel(are_l_ref, aim_l_ref, ls_l_ref, are_s_ref, aim_s_ref, ls_s_ref,
                     btr_ref, bti_ref, ctr_ref, cti_ref,
                     kt_ref, ftr_ref, fti_ref, etr_ref, eti_ref, lr_ref, li_ref, *, scan_steps):
    are = are_l_ref[...]
    aim = aim_l_ref[...]
    dt = jnp.exp(ls_l_ref[...])
    lre = are * dt
    th = aim * dt
    mag = jnp.exp(lre)
    lam_r = mag * jnp.cos(th)
    lam_i = mag * jnp.sin(th)
    den = are * are + aim * aim
    n_r = lam_r - 1.0
    n_i = lam_i
    f_r = (n_r * are + n_i * aim) / den
    f_i = (n_i * are - n_r * aim) / den
    btr = btr_ref[...]
    bti = bti_ref[...]
    bb_r = f_r * btr - f_i * bti
    bb_i = f_r * bti + f_i * btr
    kk = (CHUNK - 1 - lax.broadcasted_iota(jnp.int32, (CHUNK, SSM_STATE), 0)).astype(F32)
    pmag = jnp.exp(kk * lre)
    pw_r = pmag * jnp.cos(kk * th)
    pw_i = pmag * jnp.sin(kk * th)
    for s in range(CHUNK):
        pr = pw_r[s:s + 1, :]
        pi = pw_i[s:s + 1, :]
        ftr_ref[s * SSM_GROUP_CH:(s + 1) * SSM_GROUP_CH, :] = bb_r * pr - bb_i * pi
        fti_ref[s * SSM_GROUP_CH:(s + 1) * SSM_GROUP_CH, :] = bb_r * pi + bb_i * pr
    cmag = jnp.exp(float(CHUNK) * lre)
    cr = cmag * jnp.cos(float(CHUNK) * th)
    ci = cmag * jnp.sin(float(CHUNK) * th)
    for k in range(scan_steps):
        lr_ref[k:k + 1, :] = cr
        li_ref[k:k + 1, :] = ci
        cr, ci = cr * cr - ci * ci, 2.0 * cr * ci
    are_s = are_s_ref[...]
    aim_s = aim_s_ref[...]
    dt_s = jnp.exp(ls_s_ref[...])
    lre_s = are_s * dt_s
    th_s = aim_s * dt_s
    kl = (lax.broadcasted_iota(jnp.int32, (1, CHUNK_W), 1) // SSM_GROUP_CH).astype(F32)
    ctr = ctr_ref[...]
    cti = cti_ref[...]

    def c_times_power(k0):
        e = kl + k0
        mg = jnp.exp(e * lre_s)
        pr = mg * jnp.cos(e * th_s)
        pi = mg * jnp.sin(e * th_s)
        return ctr * pr - cti * pi, ctr * pi + cti * pr

    w0_r, w0_i = c_times_power(0.0)
    hi = lax.Precision.HIGHEST
    kt_ref[...] = (jnp.dot(bb_r, w0_r, precision=hi, preferred_element_type=F32)
                   - jnp.dot(bb_i, w0_i, precision=hi, preferred_element_type=F32))
    w1_r, w1_i = c_times_power(1.0)
    etr_ref[...] = w1_r
    eti_ref[...] = -w1_i


def ssm_prep(a_re, a_im, log_step, b_re, b_im, c_re, c_im, scan_steps):
    g, p, h = SSM_GROUPS, SSM_STATE, SSM_GROUP_CH
    ls = log_step.astype(F32)
    lane = lambda a: a.astype(F32).reshape(g, 1, p)
    sub = lambda a: a.astype(F32).reshape(g, p, 1)
    ls_l = jnp.broadcast_to(ls[:, None, None], (g, 1, p))
    ls_s = jnp.broadcast_to(ls[:, None, None], (g, p, 1))
    bt = lambda b: jnp.swapaxes(b.astype(F32), 1, 2)
    ct = lambda c: jnp.tile(jnp.swapaxes(c.astype(F32), 1, 2), (1, 1, CHUNK))
    spec3 = lambda s1, s2: pl.BlockSpec((None, s1, s2), lambda i: (i, 0, 0))
    outs = pl.pallas_call(
        functools.partial(_ssm_prep_kernel, scan_steps=scan_steps),
        grid=(g,),
        in_specs=[spec3(1, p), spec3(1, p), spec3(1, p), spec3(p, 1), spec3(p, 1), spec3(p, 1),
                  spec3(h, p), spec3(h, p), spec3(p, CHUNK_W), spec3(p, CHUNK_W)],
        out_specs=[spec3(h, CHUNK_W), spec3(CHUNK_W, p), spec3(CHUNK_W, p),
                   spec3(p, CHUNK_W), spec3(p, CHUNK_W), spec3(scan_steps, p), spec3(scan_steps, p)],
        out_shape=[jax.ShapeDtypeStruct((g, h, CHUNK_W), F32),
                   jax.ShapeDtypeStruct((g, CHUNK_W, p), F32),
                   jax.ShapeDtypeStruct((g, CHUNK_W, p), F32),
                   jax.ShapeDtypeStruct((g, p, CHUNK_W), F32),
                   jax.ShapeDtypeStruct((g, p, CHUNK_W), F32),
                   jax.ShapeDtypeStruct((g, scan_steps, p), F32),
                   jax.ShapeDtypeStruct((g, scan_steps, p), F32)],
        compiler_params=_cparams(("arbitrary",)),
        name="ssm_prep",
    )(lane(a_re), lane(a_im), ls_l, sub(a_re), sub(a_im), ls_s,
      bt(b_re), bt(b_im), ct(c_re), ct(c_im))
    kt, ftr, fti, etr, eti, lr, li = outs
    gl = SSM_LANE_GROUPS
    nb = g // gl
    p2 = 2 * p
    same = jnp.eye(gl, dtype=bool)
    kt5 = kt.reshape(nb, gl, h, CHUNK, h)[:, :, :, ::-1, :]
    kst = jnp.where(same[None, None, :, None, :, None],
                    kt5.transpose(0, 3, 1, 2, 4)[:, :, :, :, None, :], 0.0)
    kst = kst.reshape(nb, CHUNK * gl * h, gl * h)
    ft5 = jnp.concatenate([ftr, fti], axis=2).reshape(nb, gl, CHUNK, h, p2)
    fb = jnp.where(same[None, None, :, None, :, None],
                   ft5.transpose(0, 2, 1, 3, 4)[:, :, :, :, None, :], 0.0)
    fb = fb.reshape(nb, CHUNK * gl * h, gl * p2)
    et5 = jnp.concatenate([etr, eti], axis=1).reshape(nb, gl, p2, CHUNK, h)
    eb = jnp.where(same[None, :, None, None, :, None], et5[:, :, :, :, None, :], 0.0)
    eb = eb.reshape(nb, gl * p2, CHUNK * gl * h)
    lanes = lambda a: a.reshape(nb, gl, scan_steps, p2).transpose(0, 2, 1, 3).reshape(nb, scan_steps, gl * p2)
    l_same = lanes(jnp.concatenate([lr, lr], axis=2))
    l_swap = lanes(jnp.concatenate([-li, li], axis=2))
    return kst.astype(BF16), fb.astype(BF16), eb.astype(BF16), l_same, l_swap


def _ssm_kernel(z_ref, kst_ref, fb_ref, eb_ref, ls_ref, lw_ref, d_ref, y_ref, u32_ref, s_ref,
                *, scan_steps):
    rows = z_ref.shape[0] // CHUNK
    lanes = z_ref.shape[1]
    p2 = 2 * SSM_STATE
    u32_ref[...] = z_ref[...].astype(F32)

    def tokens(t):
        return u32_ref[pl.ds(t, rows, stride=CHUNK), :]

    u_all = jnp.concatenate([tokens(t).astype(BF16) for t in range(CHUNK)], axis=1)
    s_ref[...] = _dot(u_all, fb_ref[...])
    intra = [_dot(u_all[:, :(t + 1) * lanes], kst_ref[(CHUNK - 1 - t) * lanes:, :]) for t in range(CHUNK)]
    ridx = lax.broadcasted_iota(jnp.int32, (rows, 1), 0)
    prevs = []
    for g in range(SSM_LANE_GROUPS):
        cols = slice(g * p2, (g + 1) * p2)
        x = s_ref[:, cols]
        for k in range(scan_steps):
            d = 1 << k
            sh = jnp.where(ridx >= d, pltpu.roll(x, d, 0), 0.0)
            x = x + ls_ref[k:k + 1, cols] * sh + lw_ref[k:k + 1, cols] * pltpu.roll(sh, SSM_STATE, 1)
        prevs.append(jnp.where(ridx >= 1, pltpu.roll(x, 1, 0), 0.0).astype(BF16))
    y_state = _dot(jnp.concatenate(prevs, axis=1), eb_ref[...])
    for t in range(CHUNK):
        y = y_state[:, t * lanes:(t + 1) * lanes] + intra[t] + d_ref[...] * tokens(t)
        y_ref[pl.ds(t, rows, stride=CHUNK), :] = y


def ssm_chunked(z, kst, fb, eb, l_same, l_swap, d_skip, scan_steps):
    m = z.shape[0]
    nb = kst.shape[0]
    lanes = SSM_LANE_GROUPS * SSM_GROUP_CH
    ucol = (Q_WIDTH + 2 * KV_WIDTH) // lanes
    wspec = lambda a: pl.BlockSpec((None,) + a.shape[1:], lambda i, b: (i, 0, 0))
    return pl.pallas_call(
        functools.partial(_ssm_kernel, scan_steps=scan_steps),
        grid=(nb, m // SEQ),
        in_specs=[pl.BlockSpec((SEQ, lanes), lambda i, b: (b, ucol + i)),
                  wspec(kst), wspec(fb), wspec(eb), wspec(l_same), wspec(l_swap),
                  pl.BlockSpec((1, lanes), lambda i, b: (0, i))],
        out_specs=pl.BlockSpec((SEQ, lanes), lambda i, b: (b, i)),
        out_shape=jax.ShapeDtypeStruct((m, SSM_WIDTH), F32),
        scratch_shapes=[pltpu.VMEM((SEQ, lanes), F32),
                        pltpu.VMEM((SEQ // CHUNK, SSM_LANE_GROUPS * 2 * SSM_STATE), F32)],
        compiler_params=_cparams(("arbitrary", "arbitrary")),
        name="ssm_chunked",
    )(z, kst, fb, eb, l_same, l_swap, d_skip.astype(F32).reshape(1, SSM_WIDTH))


def _glu_kernel(y_ref, w_ref, b_ref, o_ref):
    y = y_ref[...]
    gel = 0.5 * y * (1.0 + jnp.tanh(math.sqrt(2.0 / math.pi) * (y + 0.044715 * (y * y * y))))
    t = _dot(gel.astype(BF16), w_ref[...]) + b_ref[...]
    o_ref[...] = (gel * (1.0 / (1.0 + jnp.exp(-t)))).astype(o_ref.dtype)


def gelu_glu(y, w_glu, b_glu):
    m = y.shape[0]
    tm = min(TM_GLU, m)
    return pl.pallas_call(
        _glu_kernel,
        grid=(m // tm,),
        in_specs=[pl.BlockSpec((tm, SSM_WIDTH), lambda i: (i, 0)),
                  pl.BlockSpec((SSM_WIDTH, SSM_WIDTH), lambda i: (0, 0)),
                  pl.BlockSpec((1, SSM_WIDTH), lambda i: (0, 0))],
        out_specs=pl.BlockSpec((tm, SSM_WIDTH), lambda i: (i, 0)),
        out_shape=jax.ShapeDtypeStruct((m, SSM_WIDTH), BF16),
        compiler_params=_cparams(("arbitrary",)),
        name="gelu_glu",
    )(y, w_glu, b_glu.reshape(1, SSM_WIDTH))


def _t5_bucket_table():
    qi = jnp.arange(BLOCK)[:, None]
    kj = jnp.arange(2 * BLOCK)[None, :]
    n = jnp.maximum(qi + BLOCK - kj, 0)
    max_exact = REL_BUCKETS // 2
    nf = jnp.maximum(n, 1).astype(F32)
    large = max_exact + (jnp.log(nf / max_exact) / math.log(REL_MAX_DIST / max_exact)
                         * (REL_BUCKETS - max_exact)).astype(jnp.int32)
    large = jnp.minimum(large, REL_BUCKETS - 1)
    return jnp.where(n < max_exact, n, large).astype(jnp.int32)


def _even_mixer(h, w_in, pool_w, pool_scale, conv_w):
    return [pool_mixer(h, w_in, pool_w, pool_scale), sconv_mixer(h, w_in, conv_w)]


def _odd_mixer(h, w_in, sinks, bias_tab, a_re, a_im, log_step, b_re, b_im, c_re, c_im,
               d_skip, w_glu, b_glu):
    scan_steps = max(1, (SEQ // CHUNK - 1).bit_length())
    z = matmul_bf16(h, w_in)
    y_attn = swa_attention(z, bias_tab, sinks)
    kst, fb, eb, l_same, l_swap = ssm_prep(a_re, a_im, log_step, b_re, b_im, c_re, c_im, scan_steps)
    y = ssm_chunked(z, kst, fb, eb, l_same, l_swap, d_skip, scan_steps)
    return [y_attn, gelu_glu(y, w_glu, b_glu)]


def kernel(x, rel_bias, norm_mix_pre, norm_mix_post, norm_ffn_pre, norm_ffn_post, e_w_in, e_pool_w, e_pool_scale, e_conv_w, e_w_out, o_w_in, o_sinks, o_a_re, o_a_im, o_log_step, o_b_re, o_b_im, o_c_re, o_c_im, o_d, o_glu_w, o_glu_b, o_w_out, f_w_up, f_conv_w, f_w_down):
    bf = lambda a: a.astype(BF16)
    xr = x.reshape(BATCH * SEQ, D_MODEL).astype(F32)
    bias_tab = rel_bias_table(rel_bias.astype(F32), _t5_bucket_table())
    h = rms_norm_bf16(xr, norm_mix_pre[0].astype(F32))
    w_up_all, w_down_all, fconv_all = bf(f_w_up), bf(f_w_down), f_conv_w.astype(F32)
    e_out_all, o_out_all = bf(e_w_out), bf(o_w_out)
    for i in range(DEPTH):
        j = i // 2
        if i % 2 == 0:
            y = _even_mixer(h, bf(e_w_in[j]), bf(e_pool_w[j]), e_pool_scale[j].astype(F32),
                            e_conv_w[j].astype(F32))
            w_out = e_out_all
        else:
            y = _odd_mixer(h, bf(o_w_in[j]), o_sinks[j].astype(F32), bias_tab,
                           o_a_re[j], o_a_im[j], o_log_step[j], o_b_re[j], o_b_im[j],
                           o_c_re[j], o_c_im[j], o_d[j], bf(o_glu_w[j]), o_glu_b[j].astype(F32))
            w_out = o_out_all
        xr, h = outproj_residual_norm(y, w_out, j, xr, norm_mix_post[i].astype(F32),
                                      norm_ffn_pre[i].astype(F32))
        act = ffn_up(h, w_up_all, fconv_all, i)
        g_next = norm_mix_pre[i + 1].astype(F32) if i + 1 < DEPTH else None
        xr, h = outproj_residual_norm([act], w_down_all, i, xr, norm_ffn_post[i].astype(F32), g_next)
    return xr.reshape(x.shape)
```
